```python
import jax, jax.numpy as jnp
from jax import lax
import numpy as np

D_MODEL = 1024
BATCH = 8
SEQ = 8192
DEPTH = 4

POOL_WINDOWS = (2, 4, 8, 16)
POOL_GROUPS = len(POOL_WINDOWS)
POOL_GROUP_DIM = D_MODEL // 8
POOL_DIM = POOL_GROUPS * POOL_GROUP_DIM
POOL_WMAX = max(POOL_WINDOWS)
HEAD_DIM = 64
N_Q_HEADS = D_MODEL // 128
N_KV_HEADS = 2
GQA_GROUP = N_Q_HEADS // N_KV_HEADS
ATTN_DIM = N_Q_HEADS * HEAD_DIM
KV_DIM = N_KV_HEADS * HEAD_DIM
WINDOW = 128
BLOCK = 128
ROPE_THETA = 500000.0
ROT_DIM = HEAD_DIM // 4
N_BRANCHES = 2
IN_DIM = POOL_DIM + ATTN_DIM + 2 * KV_DIM + N_BRANCHES * D_MODEL
D_FF = 2816
EPS = 1e-6

kernel_name = "hybrid_pool_swa_macaron"


def rmsnorm(x, g):
    xf = x.astype(jnp.float32)
    y = xf * lax.rsqrt(jnp.mean(xf * xf, axis=-1, keepdims=True) + EPS)
    return (y * g.astype(jnp.float32)).astype(x.dtype)


def swiglu(h, w_gu, w_down):
    g, u = jnp.split(h @ w_gu, 2, axis=-1)
    return (jax.nn.silu(g) * u) @ w_down


def pool_mixer(u, w_grp, scale):
    B, S, _ = u.shape
    uf = u.astype(jnp.float32)
    c = jnp.cumsum(uf, axis=1)
    c_pad = jnp.pad(c, ((0, 0), (POOL_WMAX, 0), (0, 0)))
    t = jnp.arange(S)
    outs = []
    for g, w in enumerate(POOL_WINDOWS):
        lo, hi = g * POOL_GROUP_DIM, (g + 1) * POOL_GROUP_DIM
        win_sum = c[:, :, lo:hi] - c_pad[:, POOL_WMAX - w:POOL_WMAX - w + S, lo:hi]
        count = jnp.minimum(t + 1, w).astype(jnp.float32)[None, :, None]
        outs.append(win_sum / count - uf[:, :, lo:hi])
    d = jnp.stack(outs, axis=2).astype(u.dtype)
    y = jnp.einsum('bsgc,gcd->bsgd', d, w_grp).reshape(B, S, POOL_DIM)
    return y * scale


def partial_rope(x, cos, sin):
    half = ROT_DIM // 2
    x1 = x[..., :half].astype(jnp.float32)
    x2 = x[..., half:ROT_DIM].astype(jnp.float32)
    c = cos[None, :, None, :]
    s = sin[None, :, None, :]
    rot = jnp.concatenate([x1 * c - x2 * s, x2 * c + x1 * s], axis=-1).astype(x.dtype)
    return jnp.concatenate([rot, x[..., ROT_DIM:]], axis=-1)


def swa_sink_attention(q, k, v, sinks):
    B, S = q.shape[0], q.shape[1]
    nb = S // BLOCK
    qb = q.reshape(B, nb, BLOCK, N_KV_HEADS, GQA_GROUP, HEAD_DIM)

    def with_prev(t):
        tb = t.reshape(B, nb, BLOCK, N_KV_HEADS, HEAD_DIM)
        prev = jnp.pad(tb[:, :-1], ((0, 0), (1, 0), (0, 0), (0, 0), (0, 0)))
        return jnp.concatenate([prev, tb], axis=2)

    kb, vb = with_prev(k), with_prev(v)
    s = jnp.einsum('bnqhgd,bnkhd->bnhgqk', qb, kb).astype(jnp.float32) * (HEAD_DIM ** -0.5)
    qi = jnp.arange(BLOCK)[:, None]
    ki = jnp.arange(2 * BLOCK)[None, :]
    diff = qi + BLOCK - ki
    band = (diff >= 0) & (diff < WINDOW)
    valid = (jnp.arange(nb)[:, None, None] > 0) | (ki[None] >= BLOCK)
    mask = band[None] & valid
    s = jnp.where(mask[None, :, None, None], s, -jnp.inf)
    sink = sinks.astype(jnp.float32).reshape(1, 1, N_KV_HEADS, GQA_GROUP, 1, 1)
    m = jnp.maximum(jnp.max(s, axis=-1, keepdims=True), sink)
    p = jnp.exp(s - m)
    denom = jnp.sum(p, axis=-1, keepdims=True) + jnp.exp(sink - m)
    p = (p / denom).astype(v.dtype)
    o = jnp.einsum('bnhgqk,bnkhd->bnqhgd', p, vb)
    return o.reshape(B, S, ATTN_DIM)


def setup_inputs(seed: int = 0) -> dict:
    key = jax.random.key(seed)
    ks = jax.random.split(key, 17)
    f32 = jnp.float32

    def w(k, shape, fan_in):
        return jax.random.normal(k, shape, f32) * (fan_in ** -0.5)

    def gain(k, shape):
        return 1.0 + 0.05 * jax.random.normal(k, shape, f32)

    L = DEPTH
    return {
        "x": jax.random.normal(ks[0], (BATCH, SEQ, D_MODEL), f32),
        "ln_ffn1": gain(ks[1], (L, D_MODEL)),
        "w_ffn1_gu": w(ks[2], (L, D_MODEL, 2 * D_FF), D_MODEL),
        "w_ffn1_down": w(ks[3], (L, D_FF, D_MODEL), D_FF),
        "ln_mix": gain(ks[4], (L, D_MODEL)),
        "w_in": w(ks[5], (L, D_MODEL, IN_DIM), D_MODEL),
        "pool_w": w(ks[6], (L, POOL_GROUPS, POOL_GROUP_DIM, POOL_GROUP_DIM), POOL_GROUP_DIM),
        "pool_scale": 1.0 + 0.1 * jax.random.normal(ks[7], (L, POOL_DIM), f32),
        "w_pool_branch": w(ks[8], (L, POOL_DIM, D_MODEL), POOL_DIM),
        "q_norm": gain(ks[9], (L, HEAD_DIM)),
        "k_norm": gain(ks[10], (L, HEAD_DIM)),
        "sinks": 0.5 * jax.random.normal(ks[11], (L, N_Q_HEADS), f32),
        "w_attn_branch": w(ks[12], (L, ATTN_DIM, D_MODEL), ATTN_DIM),
        "w_out": w(ks[13], (L, D_MODEL, D_MODEL), D_MODEL),
        "ln_ffn2": gain(ks[14], (L, D_MODEL)),
        "w_ffn2_gu": w(ks[15], (L, D_MODEL, 2 * D_FF), D_MODEL),
        "w_ffn2_down": w(ks[16], (L, D_FF, D_MODEL), D_FF),
    }


def reference(x, ln_ffn1, w_ffn1_gu, w_ffn1_down, ln_mix, w_in, pool_w, pool_scale,
              w_pool_branch, q_norm, k_norm, sinks, w_attn_branch, w_out,
              ln_ffn2, w_ffn2_gu, w_ffn2_down):
    B, S, _ = x.shape
    pos = jnp.arange(S, dtype=jnp.float32)
    inv_freq = ROPE_THETA ** (-jnp.arange(0, ROT_DIM, 2, dtype=jnp.float32) / ROT_DIM)
    ang = pos[:, None] * inv_freq[None, :]
    cos, sin = jnp.cos(ang), jnp.sin(ang)
    splits = [POOL_DIM, POOL_DIM + ATTN_DIM, POOL_DIM + ATTN_DIM + KV_DIM,
              POOL_DIM + ATTN_DIM + 2 * KV_DIM]

    for l in range(DEPTH):
        x = x + 0.5 * swiglu(rmsnorm(x, ln_ffn1[l]), w_ffn1_gu[l], w_ffn1_down[l])

        h = rmsnorm(x, ln_mix[l])
        z = h @ w_in[l]
        u_pool, q, k, v, gate_logits = jnp.split(z, splits, axis=-1)

        a = pool_mixer(u_pool, pool_w[l], pool_scale[l]) @ w_pool_branch[l]

        q = rmsnorm(q.reshape(B, S, N_Q_HEADS, HEAD_DIM), q_norm[l])
        k = rmsnorm(k.reshape(B, S, N_KV_HEADS, HEAD_DIM), k_norm[l])
        v = v.reshape(B, S, N_KV_HEADS, HEAD_DIM)
        q = partial_rope(q, cos, sin)
        k = partial_rope(k, cos, sin)
        b = swa_sink_attention(q, k, v, sinks[l]) @ w_attn_branch[l]

        g_pool, g_attn = jnp.split(jax.nn.sigmoid(gate_logits), N_BRANCHES, axis=-1)
        x = x + (g_pool * a + g_attn * b) @ w_out[l]

        x = x + 0.5 * swiglu(rmsnorm(x, ln_ffn2[l]), w_ffn2_gu[l], w_ffn2_down[l])
    return x
```

```python
import functools

import jax
import jax.numpy as jnp
from jax import lax
from jax.experimental import pallas as pl
from jax.experimental.pallas import tpu as pltpu

D_MODEL = 1024
D_FF = 2816
POOL_WINDOWS = (2, 4, 8, 16)
POOL_WMAX = max(POOL_WINDOWS)
POOL_DIM = 512
HEAD_DIM = 64
N_Q_HEADS = 8
N_KV_HEADS = 2
GQA_GROUP = N_Q_HEADS // N_KV_HEADS
ATTN_DIM = N_Q_HEADS * HEAD_DIM
KV_DIM = N_KV_HEADS * HEAD_DIM
BLOCK = 128
ROPE_THETA = 500000.0
ROT_DIM = HEAD_DIM // 4
EPS = 1e-6

LANES = 128
MXU_DIM = 256
VMEM_LIMIT_BYTES = 56 * 1024 * 1024

FFN_TM = 512
FFN_FC = MXU_DIM
MIX_TM = 256

F32 = jnp.float32
BF16 = jnp.bfloat16


def _dot(a, b):
    return jnp.dot(a, b, preferred_element_type=F32)


def _rms_rows(x, gain):
    ms = jnp.mean(x * x, axis=-1, keepdims=True)
    return x * lax.rsqrt(ms + EPS) * gain


def _sigmoid(x):
    return 1.0 / (1.0 + jnp.exp(-x))


def _ffn_kernel(x_ref, ln_ref, wg_ref, wu_ref, wd_ref, o_ref, act_ref):
    x = x_ref[...]
    h = _rms_rows(x, ln_ref[...]).astype(BF16)
    for c in range(D_FF // FFN_FC):
        cols = slice(c * FFN_FC, (c + 1) * FFN_FC)
        g = _dot(h, wg_ref[:, cols])
        u = _dot(h, wu_ref[:, cols])
        act_ref[:, cols] = (g * _sigmoid(g) * u).astype(BF16)
    y = _dot(act_ref[...], wd_ref[...])
    o_ref[...] = x + 0.5 * y


def _resident(shape):
    return pl.BlockSpec(shape, lambda *_: (0,) * len(shape), pipeline_mode=pl.Buffered(1))


def _ffn(x2d, ln, wg, wu, wd):
    n_tok = x2d.shape[0]
    return pl.pallas_call(
        _ffn_kernel,
        out_shape=jax.ShapeDtypeStruct(x2d.shape, F32),
        grid=(n_tok // FFN_TM,),
        in_specs=[
            pl.BlockSpec((FFN_TM, D_MODEL), lambda i: (i, 0)),
            _resident((1, D_MODEL)),
            _resident((D_MODEL, D_FF)),
            _resident((D_MODEL, D_FF)),
            _resident((D_FF, D_MODEL)),
        ],
        out_specs=pl.BlockSpec((FFN_TM, D_MODEL), lambda i: (i, 0)),
        scratch_shapes=[pltpu.VMEM((FFN_TM, D_FF), BF16)],
        compiler_params=pltpu.CompilerParams(
            dimension_semantics=("arbitrary",), vmem_limit_bytes=VMEM_LIMIT_BYTES),
        name="swiglu_halfstep",
    )(x2d, ln, wg, wu, wd)


def _head_rms(xq, gain, ones_blockdiag):
    x2 = xq * xq
    hi = x2.astype(BF16)
    lo = (x2 - hi.astype(F32)).astype(BF16)
    ss = _dot(hi, ones_blockdiag) + _dot(lo, ones_blockdiag)
    return xq * lax.rsqrt(ss * (1.0 / HEAD_DIM) + EPS) * gain


def _mixer_kernel(sinks_ref, x_ref, ln_ref, wp_ref, wq_ref, wkv_ref, wgate_ref, poolw_ref,
                  pscale_ref, wpb_ref, qg_ref, kg_ref, rope_ref, ones_ref, wab_ref, wout_ref,
                  o_ref, uext_ref, kd_ref, vd_ref, qs_ref, ao_ref):
    tm = MIX_TM
    n_blk = tm // BLOCK
    t = pl.program_id(1)

    @pl.when(t == 0)
    def _():
        uext_ref[0:POOL_WMAX, :] = jnp.zeros((POOL_WMAX, POOL_DIM), F32)
        kd_ref[:, 0:BLOCK, :] = jnp.zeros((N_KV_HEADS, BLOCK, LANES), BF16)
        vd_ref[:, 0:BLOCK, :] = jnp.zeros((N_KV_HEADS, BLOCK, LANES), BF16)

    x = x_ref[...]
    h = _rms_rows(x, ln_ref[...]).astype(BF16)
    u = _dot(h, wp_ref[...])
    q = _dot(h, wq_ref[...])
    kv = _dot(h, wkv_ref[...])
    gate_logits = _dot(h, wgate_ref[...])

    uext_ref[POOL_WMAX:POOL_WMAX + tm, :] = u
    tok = t * tm + lax.broadcasted_iota(jnp.int32, (tm, 1), 0)
    d_groups = []
    for g, w in enumerate(POOL_WINDOWS):
        lanes = slice(g * LANES, (g + 1) * LANES)
        ug = u[:, lanes]
        win = ug
        for k in range(1, w):
            win = win + uext_ref[POOL_WMAX - k:POOL_WMAX - k + tm, lanes]
        inv_count = 1.0 / jnp.minimum(tok + 1, w).astype(F32)
        d_groups.append(win * inv_count - ug)
    uext_ref[0:POOL_WMAX, :] = uext_ref[tm:tm + POOL_WMAX, :]
    y_pairs = []
    for pair in range(2):
        d_pair = jnp.concatenate(d_groups[2 * pair:2 * pair + 2], axis=1).astype(BF16)
        y_pairs.append(_dot(d_pair, poolw_ref[pair]))
    y = jnp.concatenate(y_pairs, axis=1) * pscale_ref[...]
    a = _dot(y.astype(BF16), wpb_ref[...])

    rope_c = rope_ref[:, 0:LANES]
    rope_s_hi = rope_ref[:, LANES:2 * LANES]
    rope_s_lo = rope_ref[:, 2 * LANES:3 * LANES]

    def rope(xc):
        return (xc * rope_c + pltpu.roll(xc, LANES - ROT_DIM // 2, 1) * rope_s_hi
                + pltpu.roll(xc, ROT_DIM // 2, 1) * rope_s_lo)

    lane = lax.broadcasted_iota(jnp.int32, (1, LANES), 1)
    low_half = lane < HEAD_DIM

    ones256 = ones_ref[...]
    k = kv[:, 0:KV_DIM]
    v = kv[:, KV_DIM:2 * KV_DIM]
    kn = rope(_head_rms(k, kg_ref[...], ones256[0:LANES, 0:LANES]))
    kn_sw = pltpu.roll(kn, HEAD_DIM, 1)
    v_sw = pltpu.roll(v, HEAD_DIM, 1)
    rows_new = slice(BLOCK, BLOCK + tm)
    kd_ref[0, rows_new, :] = jnp.where(low_half, kn, kn_sw).astype(BF16)
    kd_ref[1, rows_new, :] = jnp.where(low_half, kn_sw, kn).astype(BF16)
    vd_ref[0, rows_new, :] = jnp.where(low_half, v, v_sw).astype(BF16)
    vd_ref[1, rows_new, :] = jnp.where(low_half, v_sw, v).astype(BF16)

    for half in range(2):
        qn = _head_rms(q[:, half * MXU_DIM:(half + 1) * MXU_DIM],
                       jnp.concatenate([qg_ref[...], qg_ref[...]], axis=1), ones256)
        for sub in range(2):
            qc = rope(qn[:, sub * LANES:(sub + 1) * LANES])
            q_lo = jnp.where(low_half, qc, 0.0).astype(BF16)
            q_hi = jnp.where(low_half, 0.0, qc).astype(BF16)
            for b in range(n_blk):
                rows = slice(b * BLOCK, (b + 1) * BLOCK)
                qs_ref[b, half, (2 * sub) * BLOCK:(2 * sub + 1) * BLOCK, :] = q_lo[rows]
                qs_ref[b, half, (2 * sub + 1) * BLOCK:(2 * sub + 2) * BLOCK, :] = q_hi[rows]

    g_rows = GQA_GROUP * BLOCK
    qi = lax.broadcasted_iota(jnp.int32, (g_rows, 2 * BLOCK), 0) & (BLOCK - 1)
    ki = lax.broadcasted_iota(jnp.int32, (g_rows, 2 * BLOCK), 1)
    band = (ki > qi) & (ki <= qi + BLOCK)
    first_key = jnp.where(t > 0, 0, BLOCK)
    head_of_row = lax.broadcasted_iota(jnp.int32, (g_rows, 1), 0) // BLOCK
    for b in range(n_blk):
        keys = slice(b * BLOCK, (b + 2) * BLOCK)
        mask = band & (ki >= first_key) if b == 0 else band
        for hk in range(N_KV_HEADS):
            sink = jnp.zeros((g_rows, 1), F32)
            for j in range(GQA_GROUP):
                sink = jnp.where(head_of_row == j, sinks_ref[hk * GQA_GROUP + j], sink)
            s = lax.dot_general(qs_ref[b, hk], kd_ref[hk, keys, :],
                                (((1,), (1,)), ((), ())), preferred_element_type=F32)
            s = jnp.where(mask, s, -jnp.inf)
            m = jnp.maximum(jnp.max(s, axis=-1, keepdims=True), sink)
            p = jnp.exp(s - m)
            denom = jnp.sum(p, axis=-1, keepdims=True) + jnp.exp(sink - m)
            o = _dot(p.astype(BF16), vd_ref[hk, keys, :]) / denom
            rows = slice(b * BLOCK, (b + 1) * BLOCK)
            for sub in range(2):
                pair = jnp.where(low_half, o[(2 * sub) * BLOCK:(2 * sub + 1) * BLOCK],
                                 o[(2 * sub + 1) * BLOCK:(2 * sub + 2) * BLOCK])
                col = (2 * hk + sub) * LANES
                ao_ref[rows, col:col + LANES] = pair.astype(BF16)
    kd_ref[:, 0:BLOCK, :] = kd_ref[:, tm:tm + BLOCK, :]
    vd_ref[:, 0:BLOCK, :] = vd_ref[:, tm:tm + BLOCK, :]

    b_attn = _dot(ao_ref[...], wab_ref[...])
    gates = _sigmoid(gate_logits)
    merged = gates[:, 0:D_MODEL] * a + gates[:, D_MODEL:2 * D_MODEL] * b_attn
    o_ref[...] = x + _dot(merged.astype(BF16), wout_ref[...])


def _mixer(x, sinks, ln, wp, wq, wkv, wgate, poolw, pscale, wpb, qg, kg, rope_tab, ones256, wab, wout):
    bsz, seq, _ = x.shape
    tm = MIX_TM
    n_blk = tm // BLOCK
    tile = pl.BlockSpec((None, tm, D_MODEL), lambda b, t: (b, t, 0))
    return pl.pallas_call(
        _mixer_kernel,
        out_shape=jax.ShapeDtypeStruct(x.shape, F32),
        grid=(bsz, seq // tm),
        in_specs=[
            pl.BlockSpec(memory_space=pltpu.SMEM),
            tile,
            _resident((1, D_MODEL)),
            _resident((D_MODEL, POOL_DIM)),
            _resident((D_MODEL, ATTN_DIM)),
            _resident((D_MODEL, 2 * KV_DIM)),
            _resident((D_MODEL, 2 * D_MODEL)),
            _resident((2, MXU_DIM, MXU_DIM)),
            _resident((1, POOL_DIM)),
            _resident((POOL_DIM, D_MODEL)),
            _resident((1, LANES)),
            _resident((1, LANES)),
            pl.BlockSpec((tm, 3 * LANES), lambda b, t: (t, 0)),
            _resident((MXU_DIM, MXU_DIM)),
            _resident((ATTN_DIM, D_MODEL)),
            _resident((D_MODEL, D_MODEL)),
        ],
        out_specs=tile,
        scratch_shapes=[
            pltpu.VMEM((POOL_WMAX + tm, POOL_DIM), F32),
            pltpu.VMEM((N_KV_HEADS, BLOCK + tm, LANES), BF16),
            pltpu.VMEM((N_KV_HEADS, BLOCK + tm, LANES), BF16),
            pltpu.VMEM((n_blk, N_KV_HEADS, GQA_GROUP * BLOCK, LANES), BF16),
            pltpu.VMEM((tm, ATTN_DIM), BF16),
        ],
        compiler_params=pltpu.CompilerParams(
            dimension_semantics=("arbitrary", "arbitrary"), vmem_limit_bytes=VMEM_LIMIT_BYTES),
        name="gated_token_mixers",
    )(sinks, x, ln, wp, wq, wkv, wgate, poolw, pscale, wpb, qg, kg, rope_tab, ones256, wab, wout)


def _rope_table(seq):
    half = ROT_DIM // 2
    pos = jnp.arange(seq, dtype=F32)
    inv_freq = ROPE_THETA ** (-jnp.arange(0, ROT_DIM, 2, dtype=F32) / ROT_DIM)
    ang = pos[:, None] * inv_freq[None, :]
    cos, sin = jnp.cos(ang), jnp.sin(ang)
    pad = HEAD_DIM - ROT_DIM
    c64 = jnp.concatenate([cos, cos, jnp.ones((seq, pad), F32)], axis=1)
    s_hi = jnp.concatenate([-sin, jnp.zeros((seq, half + pad), F32)], axis=1)
    s_lo = jnp.concatenate([jnp.zeros((seq, half), F32), sin, jnp.zeros((seq, pad), F32)], axis=1)
    return jnp.concatenate([c64, c64, s_hi, s_hi, s_lo, s_lo], axis=1)


def _block_diag_pairs(pool_w):
    z = jnp.zeros((LANES, LANES), pool_w.dtype)
    mats = []
    for pair in range(2):
        w0, w1 = pool_w[2 * pair], pool_w[2 * pair + 1]
        mats.append(jnp.concatenate([jnp.concatenate([w0, z], axis=1),
                                     jnp.concatenate([z, w1], axis=1)], axis=0))
    return jnp.stack(mats)


def kernel(x, ln_ffn1, w_ffn1_gu, w_ffn1_down, ln_mix, w_in, pool_w, pool_scale, w_pool_branch,
           q_norm, k_norm, sinks, w_attn_branch, w_out, ln_ffn2, w_ffn2_gu, w_ffn2_down):
    bsz, seq, _ = x.shape
    depth = ln_ffn1.shape[0]
    assert seq % MIX_TM == 0 and (bsz * seq) % FFN_TM == 0

    rope_tab = _rope_table(seq)
    head_id = jnp.arange(MXU_DIM) // HEAD_DIM
    ones256 = (head_id[:, None] == head_id[None, :]).astype(BF16)
    c_pool, c_q, c_kv = POOL_DIM, POOL_DIM + ATTN_DIM, POOL_DIM + ATTN_DIM + 2 * KV_DIM

    for l in range(depth):
        x = _ffn(x.reshape(bsz * seq, D_MODEL), ln_ffn1[l][None, :],
                 w_ffn1_gu[l, :, :D_FF].astype(BF16), w_ffn1_gu[l, :, D_FF:].astype(BF16),
                 w_ffn1_down[l].astype(BF16)).reshape(bsz, seq, D_MODEL)
        w_in_l = w_in[l].astype(BF16)
        qg = jnp.tile(q_norm[l] * (HEAD_DIM ** -0.5), LANES // HEAD_DIM)[None, :]
        kg = jnp.tile(k_norm[l], LANES // HEAD_DIM)[None, :]
        x = _mixer(x, sinks[l], ln_mix[l][None, :],
                   w_in_l[:, :c_pool], w_in_l[:, c_pool:c_q], w_in_l[:, c_q:c_kv], w_in_l[:, c_kv:],
                   _block_diag_pairs(pool_w[l]).astype(BF16), pool_scale[l][None, :],
                   w_pool_branch[l].astype(BF16), qg, kg, rope_tab, ones256,
                   w_attn_branch[l].astype(BF16), w_out[l].astype(BF16))
        x = _ffn(x.reshape(bsz * seq, D_MODEL), ln_ffn2[l][None, :],
                 w_ffn2_gu[l, :, :D_FF].astype(BF16), w_ffn2_gu[l, :, D_FF:].astype(BF16),
                 w_ffn2_down[l].astype(BF16)).reshape(bsz, seq, D_MODEL)
    return x
```

```python
import jax
import jax.numpy as jnp
from jax import lax
from jax.experimental import pallas as pl
from jax.experimental.pallas import tpu as pltpu

D_MODEL = 1024
D_FF = 2816
POOL_WINDOWS = (2, 4, 8, 16)
POOL_WMAX = max(POOL_WINDOWS)
POOL_DIM = 512
HEAD_DIM = 64
N_Q_HEADS = 8
N_KV_HEADS = 2
GQA_GROUP = N_Q_HEADS // N_KV_HEADS
ATTN_DIM = N_Q_HEADS * HEAD_DIM
KV_DIM = N_KV_HEADS * HEAD_DIM
IN_DIM = POOL_DIM + ATTN_DIM + 2 * KV_DIM + 2 * D_MODEL
BLOCK = 128
ROPE_THETA = 500000.0
ROT_DIM = HEAD_DIM // 4
EPS = 1e-6

LANES = 128
MXU_DIM = 256
VMEM_LIMIT_BYTES = 56 * 1024 * 1024

FFN_TM = 1024
FFN_FC = MXU_DIM
MIX_TM = 512

F32 = jnp.float32
BF16 = jnp.bfloat16


def _dot(a, b):
    return jnp.dot(a, b, preferred_element_type=F32)


def _rms_rows(x, gain):
    ms = jnp.mean(x * x, axis=-1, keepdims=True)
    return x * lax.rsqrt(ms + EPS) * gain


def _sigmoid(x):
    return 1.0 / (1.0 + jnp.exp(-x))


def _resident(shape):
    return pl.BlockSpec(shape, lambda *_: (0,) * len(shape), pipeline_mode=pl.Buffered(1))


def _layer(layer, shape):
    return pl.BlockSpec((None,) + shape, lambda *_: (layer,) + (0,) * len(shape),
                        pipeline_mode=pl.Buffered(1))


def _ffn_kernel(x_ref, ln_ref, wgu_ref, wd_ref, o_ref, act_ref):
    x = x_ref[...]
    h = _rms_rows(x, ln_ref[...]).astype(BF16)
    for c in range(D_FF // FFN_FC):
        cols = slice(c * FFN_FC, (c + 1) * FFN_FC)
        g = _dot(h, wgu_ref[:, cols])
        u = _dot(h, wgu_ref[:, D_FF + c * FFN_FC:D_FF + (c + 1) * FFN_FC])
        act_ref[:, cols] = (g * _sigmoid(g) * u).astype(BF16)
    y = _dot(act_ref[...], wd_ref[...])
    o_ref[...] = x + 0.5 * y


def _ffn(x2d, layer, ln, wgu, wd):
    n_tok = x2d.shape[0]
    return pl.pallas_call(
        _ffn_kernel,
        out_shape=jax.ShapeDtypeStruct(x2d.shape, F32),
        grid=(n_tok // FFN_TM,),
        in_specs=[
            pl.BlockSpec((FFN_TM, D_MODEL), lambda i: (i, 0)),
            _layer(layer, (1, D_MODEL)),
            _layer(layer, (D_MODEL, 2 * D_FF)),
            _layer(layer, (D_FF, D_MODEL)),
        ],
        out_specs=pl.BlockSpec((FFN_TM, D_MODEL), lambda i: (i, 0)),
        scratch_shapes=[pltpu.VMEM((FFN_TM, D_FF), BF16)],
        compiler_params=pltpu.CompilerParams(
            dimension_semantics=("arbitrary",), vmem_limit_bytes=VMEM_LIMIT_BYTES),
        name="swiglu_halfstep",
    )(x2d, ln, wgu, wd)


def _head_rms(xq, gain, mean_blockdiag):
    x2 = xq * xq
    hi = x2.astype(BF16)
    lo = (x2 - hi.astype(F32)).astype(BF16)
    ms = _dot(jnp.concatenate([hi, lo], axis=1), mean_blockdiag)
    return xq * lax.rsqrt(ms + EPS) * gain


def _mixer_kernel(layer, sinks_ref, x_ref, ln_ref, win_ref, poolw_ref, pscale_ref, wpb_ref,
                  qg_ref, kg_ref, rope_ref, mean_ref, wab_ref, wout_ref,
                  o_ref, uext_ref, kd_ref, vd_ref, qs_ref, ao_ref):
    tm = MIX_TM
    n_blk = tm // BLOCK
    t = pl.program_id(1)
    c_q, c_kv, c_gate = POOL_DIM, POOL_DIM + ATTN_DIM, POOL_DIM + ATTN_DIM + 2 * KV_DIM

    @pl.when(t == 0)
    def _():
        uext_ref[0:POOL_WMAX, :] = jnp.zeros((POOL_WMAX, POOL_DIM), F32)
        kd_ref[:, 0:BLOCK, :] = jnp.zeros((N_KV_HEADS, BLOCK, LANES), BF16)
        vd_ref[:, 0:BLOCK, :] = jnp.zeros((N_KV_HEADS, BLOCK, LANES), BF16)

    x = x_ref[...]
    h = _rms_rows(x, ln_ref[...]).astype(BF16)
    u = _dot(h, win_ref[:, 0:c_q])
    q = _dot(h, win_ref[:, c_q:c_kv])
    kv = _dot(h, win_ref[:, c_kv:c_gate])
    gate_logits = _dot(h, win_ref[:, c_gate:IN_DIM])

    uext_ref[POOL_WMAX:POOL_WMAX + tm, :] = u
    tok = t * tm + lax.broadcasted_iota(jnp.int32, (tm, 1), 0)
    d_groups = []
    for g, w in enumerate(POOL_WINDOWS):
        lanes = slice(g * LANES, (g + 1) * LANES)
        win = uext_ref[:, lanes]
        span = 1
        while span < w:
            win = win + pltpu.roll(win, span, 0)
            span *= 2
        inv_count = 1.0 / jnp.minimum(tok + 1, w).astype(F32)
        d_groups.append(win[POOL_WMAX:] * inv_count - u[:, lanes])
    uext_ref[0:POOL_WMAX, :] = uext_ref[tm:tm + POOL_WMAX, :]
    y_pairs = []
    for pair in range(2):
        d_pair = jnp.concatenate(d_groups[2 * pair:2 * pair + 2], axis=1).astype(BF16)
        y_pairs.append(_dot(d_pair, poolw_ref[pair]))
    y = jnp.concatenate(y_pairs, axis=1) * pscale_ref[...]
    a = _dot(y.astype(BF16), wpb_ref[...])

    rope_c = rope_ref[:, 0:LANES]
    rope_s_hi = rope_ref[:, LANES:2 * LANES]
    rope_s_lo = rope_ref[:, 2 * LANES:3 * LANES]

    def rope(xc):
        return (xc * rope_c + pltpu.roll(xc, LANES - ROT_DIM // 2, 1) * rope_s_hi
                + pltpu.roll(xc, ROT_DIM // 2, 1) * rope_s_lo)

    lane = lax.broadcasted_iota(jnp.int32, (1, LANES), 1)
    low_half = lane < HEAD_DIM

    k = kv[:, 0:KV_DIM]
    v = kv[:, KV_DIM:2 * KV_DIM]
    mean_k = mean_ref[0:LANES, 0:LANES]
    kn = rope(_head_rms(k, kg_ref[...], jnp.concatenate([mean_k, mean_k], axis=0)))
    kn_sw = pltpu.roll(kn, HEAD_DIM, 1)
    v_sw = pltpu.roll(v, HEAD_DIM, 1)
    rows_new = slice(BLOCK, BLOCK + tm)
    kd_ref[0, rows_new, :] = jnp.where(low_half, kn, kn_sw).astype(BF16)
    kd_ref[1, rows_new, :] = jnp.where(low_half, kn_sw, kn).astype(BF16)
    vd_ref[0, rows_new, :] = jnp.where(low_half, v, v_sw).astype(BF16)
    vd_ref[1, rows_new, :] = jnp.where(low_half, v_sw, v).astype(BF16)

    for half in range(2):
        qn = _head_rms(q[:, half * MXU_DIM:(half + 1) * MXU_DIM], qg_ref[...], mean_ref[...])
        for sub in range(2):
            qc = rope(qn[:, sub * LANES:(sub + 1) * LANES])
            q_lo = jnp.where(low_half, qc, 0.0).astype(BF16)
            q_hi = jnp.where(low_half, 0.0, qc).astype(BF16)
            for b in range(n_blk):
                rows = slice(b * BLOCK, (b + 1) * BLOCK)
                qs_ref[b, half, (2 * sub) * BLOCK:(2 * sub + 1) * BLOCK, :] = q_lo[rows]
                qs_ref[b, half, (2 * sub + 1) * BLOCK:(2 * sub + 2) * BLOCK, :] = q_hi[rows]

    g_rows = GQA_GROUP * BLOCK
    qi = lax.broadcasted_iota(jnp.int32, (g_rows, 2 * BLOCK), 0) & (BLOCK - 1)
    ki = lax.broadcasted_iota(jnp.int32, (g_rows, 2 * BLOCK), 1)
    band = (ki > qi) & (ki <= qi + BLOCK)
    first_key = jnp.where(t > 0, 0, BLOCK)
    head_of_row = lax.broadcasted_iota(jnp.int32, (g_rows, 1), 0) // BLOCK
    for b in range(n_blk):
        keys = slice(b * BLOCK, (b + 2) * BLOCK)
        mask = band & (ki >= first_key) if b == 0 else band
        for hk in range(N_KV_HEADS):
            sink = jnp.zeros((g_rows, 1), F32)
            for j in range(GQA_GROUP):
                sink = jnp.where(head_of_row == j, sinks_ref[layer, hk * GQA_GROUP + j], sink)
            s = lax.dot_general(qs_ref[b, hk], kd_ref[hk, keys, :],
                                (((1,), (1,)), ((), ())), preferred_element_type=F32)
            s = jnp.where(mask, s, -jnp.inf)
            m = jnp.maximum(jnp.max(s, axis=-1, keepdims=True), sink)
            p = jnp.exp(s - m)
            denom = jnp.sum(p, axis=-1, keepdims=True) + jnp.exp(sink - m)
            o = _dot(p.astype(BF16), vd_ref[hk, keys, :]) / denom
            rows = slice(b * BLOCK, (b + 1) * BLOCK)
            for sub in range(2):
                pair = jnp.where(low_half, o[(2 * sub) * BLOCK:(2 * sub + 1) * BLOCK],
                                 o[(2 * sub + 1) * BLOCK:(2 * sub + 2) * BLOCK])
                col = (2 * hk + sub) * LANES
                ao_ref[rows, col:col + LANES] = pair.astype(BF16)
    kd_ref[:, 0:BLOCK, :] = kd_ref[:, tm:tm + BLOCK, :]
    vd_ref[:, 0:BLOCK, :] = vd_ref[:, tm:tm + BLOCK, :]

    b_attn = _dot(ao_ref[...], wab_ref[...])
    gates = _sigmoid(gate_logits)
    merged = gates[:, 0:D_MODEL] * a + gates[:, D_MODEL:2 * D_MODEL] * b_attn
    o_ref[...] = x + _dot(merged.astype(BF16), wout_ref[...])


def _mixer(x, layer, sinks, ln, win, poolw, pscale, wpb, qg, kg, rope_tab, mean_bd, wab, wout):
    bsz, seq, _ = x.shape
    tm = MIX_TM
    n_blk = tm // BLOCK
    tile = pl.BlockSpec((None, tm, D_MODEL), lambda b, t: (b, t, 0))
    return pl.pallas_call(
        lambda *refs: _mixer_kernel(layer, *refs),
        out_shape=jax.ShapeDtypeStruct(x.shape, F32),
        grid=(bsz, seq // tm),
        in_specs=[
            pl.BlockSpec(memory_space=pltpu.SMEM),
            tile,
            _layer(layer, (1, D_MODEL)),
            _layer(layer, (D_MODEL, IN_DIM)),
            _layer(layer, (2, MXU_DIM, MXU_DIM)),
            _layer(layer, (1, POOL_DIM)),
            _layer(layer, (POOL_DIM, D_MODEL)),
            _layer(layer, (1, MXU_DIM)),
            _layer(layer, (1, LANES)),
            pl.BlockSpec((tm, 3 * LANES), lambda b, t: (t, 0)),
            _resident((2 * MXU_DIM, MXU_DIM)),
            _layer(layer, (ATTN_DIM, D_MODEL)),
            _layer(layer, (D_MODEL, D_MODEL)),
        ],
        out_specs=tile,
        scratch_shapes=[
            pltpu.VMEM((POOL_WMAX + tm, POOL_DIM), F32),
            pltpu.VMEM((N_KV_HEADS, BLOCK + tm, LANES), BF16),
            pltpu.VMEM((N_KV_HEADS, BLOCK + tm, LANES), BF16),
            pltpu.VMEM((n_blk, N_KV_HEADS, GQA_GROUP * BLOCK, LANES), BF16),
            pltpu.VMEM((tm, ATTN_DIM), BF16),
        ],
        compiler_params=pltpu.CompilerParams(
            dimension_semantics=("arbitrary", "arbitrary"), vmem_limit_bytes=VMEM_LIMIT_BYTES),
        name="gated_token_mixers",
    )(sinks, x, ln, win, poolw, pscale, wpb, qg, kg, rope_tab, mean_bd, wab, wout)


def _rope_table(seq):
    half = ROT_DIM // 2
    pos = jnp.arange(seq, dtype=F32)
    inv_freq = ROPE_THETA ** (-jnp.arange(0, ROT_DIM, 2, dtype=F32) / ROT_DIM)
    ang = pos[:, None] * inv_freq[None, :]
    cos, sin = jnp.cos(ang), jnp.sin(ang)
    pad = HEAD_DIM - ROT_DIM
    c64 = jnp.concatenate([cos, cos, jnp.ones((seq, pad), F32)], axis=1)
    s_hi = jnp.concatenate([-sin, jnp.zeros((seq, half + pad), F32)], axis=1)
    s_lo = jnp.concatenate([jnp.zeros((seq, half), F32), sin, jnp.zeros((seq, pad), F32)], axis=1)
    return jnp.concatenate([c64, c64, s_hi, s_hi, s_lo, s_lo], axis=1)


def _block_diag_pairs(pool_w):
    depth = pool_w.shape[0]
    out = jnp.zeros((depth, 2, MXU_DIM, MXU_DIM), pool_w.dtype)
    out = out.at[:, :, :LANES, :LANES].set(pool_w[:, 0::2])
    return out.at[:, :, LANES:, LANES:].set(pool_w[:, 1::2])


def kernel(x, ln_ffn1, w_ffn1_gu, w_ffn1_down, ln_mix, w_in, pool_w, pool_scale, w_pool_branch,
           q_norm, k_norm, sinks, w_attn_branch, w_out, ln_ffn2, w_ffn2_gu, w_ffn2_down):
    bsz, seq, _ = x.shape
    depth = ln_ffn1.shape[0]
    assert seq % MIX_TM == 0 and (bsz * seq) % FFN_TM == 0

    rope_tab = _rope_table(seq)
    head_id = (jnp.arange(2 * MXU_DIM) % MXU_DIM) // HEAD_DIM
    mean_bd = ((head_id[:, None] == head_id[None, :MXU_DIM]) * (1.0 / HEAD_DIM)).astype(BF16)

    wgu1, wd1 = w_ffn1_gu.astype(BF16), w_ffn1_down.astype(BF16)
    wgu2, wd2 = w_ffn2_gu.astype(BF16), w_ffn2_down.astype(BF16)
    win, wpb = w_in.astype(BF16), w_pool_branch.astype(BF16)
    wab, wout = w_attn_branch.astype(BF16), w_out.astype(BF16)
    poolw = _block_diag_pairs(pool_w).astype(BF16)
    ln1, lnm, ln2 = ln_ffn1[:, None, :], ln_mix[:, None, :], ln_ffn2[:, None, :]
    pscale = pool_scale[:, None, :]
    qg = jnp.tile(q_norm * (HEAD_DIM ** -0.5), (1, MXU_DIM // HEAD_DIM))[:, None, :]
    kg = jnp.tile(k_norm, (1, LANES // HEAD_DIM))[:, None, :]

    x2d = x.reshape(bsz * seq, D_MODEL)
    for l in range(depth):
        x2d = _ffn(x2d, l, ln1, wgu1, wd1)
        x2d = _mixer(x2d.reshape(bsz, seq, D_MODEL), l, sinks, lnm, win, poolw, pscale, wpb,
                     qg, kg, rope_tab, mean_bd, wab, wout).reshape(bsz * seq, D_MODEL)
        x2d = _ffn(x2d, l, ln2, wgu2, wd2)
    return x2d.reshape(bsz, seq, D_MODEL)
```

```python
import jax
import jax.numpy as jnp
from jax import lax
from jax.experimental import pallas as pl
from jax.experimental.pallas import tpu as pltpu

D_MODEL = 1024
D_FF = 2816
POOL_WINDOWS = (2, 4, 8, 16)
POOL_WMAX = max(POOL_WINDOWS)
POOL_DIM = 512
HEAD_DIM = 64
N_Q_HEADS = 8
N_KV_HEADS = 2
GQA_GROUP = N_Q_HEADS // N_KV_HEADS
ATTN_DIM = N_Q_HEADS * HEAD_DIM
KV_DIM = N_KV_HEADS * HEAD_DIM
IN_DIM = POOL_DIM + ATTN_DIM + 2 * KV_DIM + 2 * D_MODEL
IN_SPLITS = (POOL_DIM, POOL_DIM + ATTN_DIM, POOL_DIM + ATTN_DIM + 2 * KV_DIM)
BLOCK = 128
ROPE_THETA = 500000.0
ROT_DIM = HEAD_DIM // 4
EPS = 1e-6

LANES = 128
MXU_DIM = 256
VMEM_LIMIT_BYTES = 56 * 1024 * 1024

FFN_TM = 1024
FFN_FC = MXU_DIM
MIX_TM = 256
MIX_PHASES = 2
IN_CHUNK = MXU_DIM

F32 = jnp.float32
BF16 = jnp.bfloat16


def _dot(a, b):
    return jnp.dot(a, b, preferred_element_type=F32)


def _rms_rows(x, gain):
    ms = jnp.mean(x * x, axis=-1, keepdims=True)
    return x * lax.rsqrt(ms + EPS) * gain


def _sigmoid(x):
    return 1.0 / (1.0 + jnp.exp(-x))


def _resident(shape):
    return pl.BlockSpec(shape, lambda *_: (0,) * len(shape), pipeline_mode=pl.Buffered(1))


def _layer(layer, shape):
    return pl.BlockSpec((None,) + shape, lambda *_: (layer,) + (0,) * len(shape),
                        pipeline_mode=pl.Buffered(1))


def _ffn_kernel(x_ref, ln_ref, wgu_ref, wd_ref, o_ref, act_ref):
    x = x_ref[...]
    h = _rms_rows(x, ln_ref[...]).astype(BF16)
    for c in range(D_FF // FFN_FC):
        cols = slice(c * FFN_FC, (c + 1) * FFN_FC)
        g = _dot(h, wgu_ref[:, cols])
        u = _dot(h, wgu_ref[:, D_FF + c * FFN_FC:D_FF + (c + 1) * FFN_FC])
        act_ref[:, cols] = (g * _sigmoid(g) * u).astype(BF16)
    y = _dot(act_ref[...], wd_ref[...])
    o_ref[...] = x + 0.5 * y


def _ffn(x2d, layer, ln, wgu, wd):
    n_tok = x2d.shape[0]
    return pl.pallas_call(
        _ffn_kernel,
        out_shape=jax.ShapeDtypeStruct(x2d.shape, F32),
        grid=(n_tok // FFN_TM,),
        in_specs=[
            pl.BlockSpec((FFN_TM, D_MODEL), lambda i: (i, 0)),
            _layer(layer, (1, D_MODEL)),
            _layer(layer, (D_MODEL, 2 * D_FF)),
            _layer(layer, (D_FF, D_MODEL)),
        ],
        out_specs=pl.BlockSpec((FFN_TM, D_MODEL), lambda i: (i, 0)),
        scratch_shapes=[pltpu.VMEM((FFN_TM, D_FF), BF16)],
        compiler_params=pltpu.CompilerParams(
            dimension_semantics=("arbitrary",), vmem_limit_bytes=VMEM_LIMIT_BYTES),
        name="swiglu_halfstep",
    )(x2d, ln, wgu, wd)


def _head_rms(xq, gain, mean_blockdiag):
    x2 = xq * xq
    hi = x2.astype(BF16)
    lo = (x2 - hi.astype(F32)).astype(BF16)
    ms = _dot(jnp.concatenate([hi, lo], axis=1), mean_blockdiag)
    return xq * lax.rsqrt(ms + EPS) * gain


def _mix_phase(layer, t, at_seq_start, x_rows, rope, o_ref, rows_out, z_ref, h_ref, znext_ref,
               normalise_ahead, sinks_ref, win_ref, poolw_ref, pscale_ref, wpb_ref, qg_ref, kg_ref,
               mean_ref, wab_ref, wout_ref, uext_ref, kd_ref, vd_ref, qs_ref, ao_ref):
    tm = MIX_TM
    n_blk = tm // BLOCK
    c_q, c_kv, c_gate = IN_SPLITS
    chunks = iter(range(IN_DIM // IN_CHUNK))

    def project_next(n):
        for _ in range(n):
            c = next(chunks)
            cols = slice(c * IN_CHUNK, (c + 1) * IN_CHUNK)
            znext_ref[:, cols] = _dot(h_ref[...], win_ref[:, cols])

    rope_c = rope[:, 0:LANES]
    rope_s_hi = rope[:, LANES:2 * LANES]
    rope_s_lo = rope[:, 2 * LANES:3 * LANES]

    def apply_rope(xc):
        return (xc * rope_c + pltpu.roll(xc, LANES - ROT_DIM // 2, 1) * rope_s_hi
                + pltpu.roll(xc, ROT_DIM // 2, 1) * rope_s_lo)

    lane = lax.broadcasted_iota(jnp.int32, (1, LANES), 1)
    low_half = lane < HEAD_DIM

    project_next(2)
    gates = _sigmoid(z_ref[:, c_gate:IN_DIM])
    project_next(1)

    u = z_ref[:, 0:c_q]
    uext_ref[POOL_WMAX:POOL_WMAX + tm, :] = u
    tok = t * tm + lax.broadcasted_iota(jnp.int32, (tm, 1), 0)
    d_groups = []
    for g, w in enumerate(POOL_WINDOWS):
        lanes = slice(g * LANES, (g + 1) * LANES)
        win = uext_ref[:, lanes]
        span = 1
        while span < w:
            win = win + pltpu.roll(win, span, 0)
            span *= 2
        inv_count = 1.0 / jnp.minimum(tok + 1, w).astype(F32)
        d_groups.append(win[POOL_WMAX:] * inv_count - u[:, lanes])
    uext_ref[0:POOL_WMAX, :] = uext_ref[tm:tm + POOL_WMAX, :]
    y_pairs = []
    for pair in range(2):
        d_pair = jnp.concatenate(d_groups[2 * pair:2 * pair + 2], axis=1).astype(BF16)
        y_pairs.append(_dot(d_pair, poolw_ref[pair]))
    y = jnp.concatenate(y_pairs, axis=1) * pscale_ref[...]
    gated_pool = gates[:, 0:D_MODEL] * _dot(y.astype(BF16), wpb_ref[...])
    project_next(1)

    k = z_ref[:, c_kv:c_kv + KV_DIM]
    v = z_ref[:, c_kv + KV_DIM:c_gate]
    mean_k = mean_ref[0:LANES, 0:LANES]
    kn = apply_rope(_head_rms(k, kg_ref[...], jnp.concatenate([mean_k, mean_k], axis=0)))
    kn_sw = pltpu.roll(kn, HEAD_DIM, 1)
    v_sw = pltpu.roll(v, HEAD_DIM, 1)
    rows_new = slice(BLOCK, BLOCK + tm)
    kd_ref[0, rows_new, :] = jnp.where(low_half, kn, kn_sw).astype(BF16)
    kd_ref[1, rows_new, :] = jnp.where(low_half, kn_sw, kn).astype(BF16)
    vd_ref[0, rows_new, :] = jnp.where(low_half, v, v_sw).astype(BF16)
    vd_ref[1, rows_new, :] = jnp.where(low_half, v_sw, v).astype(BF16)

    for half in range(2):
        qn = _head_rms(z_ref[:, c_q + half * MXU_DIM:c_q + (half + 1) * MXU_DIM],
                       qg_ref[...], mean_ref[...])
        for sub in range(2):
            qc = apply_rope(qn[:, sub * LANES:(sub + 1) * LANES])
            q_lo = jnp.where(low_half, qc, 0.0).astype(BF16)
            q_hi = jnp.where(low_half, 0.0, qc).astype(BF16)
            for b in range(n_blk):
                rows = slice(b * BLOCK, (b + 1) * BLOCK)
                qs_ref[b, half, (2 * sub) * BLOCK:(2 * sub + 1) * BLOCK, :] = q_lo[rows]
                qs_ref[b, half, (2 * sub + 1) * BLOCK:(2 * sub + 2) * BLOCK, :] = q_hi[rows]
        project_next(1)

    g_rows = GQA_GROUP * BLOCK
    qi = lax.broadcasted_iota(jnp.int32, (g_rows, 2 * BLOCK), 0) & (BLOCK - 1)
    ki = lax.broadcasted_iota(jnp.int32, (g_rows, 2 * BLOCK), 1)
    band = (ki > qi) & (ki <= qi + BLOCK)
    head_of_row = lax.broadcasted_iota(jnp.int32, (g_rows, 1), 0) // BLOCK
    for b in range(n_blk):
        keys = slice(b * BLOCK, (b + 2) * BLOCK)
        if b == 0 and at_seq_start:
            mask = band & (ki >= jnp.where(t > 0, 0, BLOCK))
        else:
            mask = band
        for hk in range(N_KV_HEADS):
            sink = jnp.zeros((g_rows, 1), F32)
            for j in range(GQA_GROUP):
                sink = jnp.where(head_of_row == j, sinks_ref[layer, hk * GQA_GROUP + j], sink)
            s = lax.dot_general(qs_ref[b, hk], kd_ref[hk, keys, :],
                                (((1,), (1,)), ((), ())), preferred_element_type=F32)
            s = jnp.where(mask, s, -jnp.inf)
            m = jnp.maximum(jnp.max(s, axis=-1, keepdims=True), sink)
            p = jnp.exp(s - m)
            denom = jnp.sum(p, axis=-1, keepdims=True) + jnp.exp(sink - m)
            o = _dot(p.astype(BF16), vd_ref[hk, keys, :]) / denom
            rows = slice(b * BLOCK, (b + 1) * BLOCK)
            for sub in range(2):
                pair = jnp.where(low_half, o[(2 * sub) * BLOCK:(2 * sub + 1) * BLOCK],
                                 o[(2 * sub + 1) * BLOCK:(2 * sub + 2) * BLOCK])
                col = (2 * hk + sub) * LANES
                ao_ref[rows, col:col + LANES] = pair.astype(BF16)
            unit = b * N_KV_HEADS + hk
            n_units = n_blk * N_KV_HEADS
            project_next((unit + 1) * 4 // n_units - unit * 4 // n_units)
    kd_ref[:, 0:BLOCK, :] = kd_ref[:, tm:tm + BLOCK, :]
    vd_ref[:, 0:BLOCK, :] = vd_ref[:, tm:tm + BLOCK, :]
    b_attn = _dot(ao_ref[...], wab_ref[...])

    merged = gated_pool + gates[:, D_MODEL:2 * D_MODEL] * b_attn
    project_next(3)
    assert next(chunks, None) is None
    normalise_ahead()
    o_ref[rows_out, :] = x_rows[...] + _dot(merged.astype(BF16), wout_ref[...])


def _mixer_kernel(layer, tiles_per_seq, sinks_ref, x_ref, xnext_ref, ln_ref, win_ref, poolw_ref,
                  pscale_ref, wpb_ref, qg_ref, kg_ref, rope_ref, mean_ref, wab_ref, wout_ref,
                  o_ref, z_ref, h_ref, uext_ref, kd_ref, vd_ref, qs_ref, ao_ref):
    tm = MIX_TM
    i = pl.program_id(0)
    t0 = lax.rem(i * MIX_PHASES, tiles_per_seq)

    def norm_rows(src_ref, ph):
        rows = slice(ph * tm, (ph + 1) * tm)
        h_ref[ph] = _rms_rows(src_ref[rows, :], ln_ref[...]).astype(BF16)

    @pl.when(t0 == 0)
    def _():
        uext_ref[0:POOL_WMAX, :] = jnp.zeros((POOL_WMAX, POOL_DIM), F32)
        kd_ref[:, 0:BLOCK, :] = jnp.zeros((N_KV_HEADS, BLOCK, LANES), BF16)
        vd_ref[:, 0:BLOCK, :] = jnp.zeros((N_KV_HEADS, BLOCK, LANES), BF16)

    @pl.when(i == 0)
    def _():
        norm_rows(x_ref, 0)
        for c in range(IN_DIM // IN_CHUNK):
            cols = slice(c * IN_CHUNK, (c + 1) * IN_CHUNK)
            z_ref[0, :, cols] = _dot(h_ref[0], win_ref[:, cols])
        norm_rows(x_ref, 1)

    for ph in range(MIX_PHASES):
        nxt = (ph + 1) % MIX_PHASES
        rows = slice(ph * tm, (ph + 1) * tm)
        _mix_phase(layer, t0 + ph, ph == 0, x_ref.at[rows], rope_ref.at[rows], o_ref, rows,
                   z_ref.at[ph], h_ref.at[nxt], z_ref.at[nxt],
                   lambda ph=ph: norm_rows(xnext_ref, ph),
                   sinks_ref, win_ref, poolw_ref, pscale_ref, wpb_ref, qg_ref, kg_ref, mean_ref,
                   wab_ref, wout_ref, uext_ref, kd_ref, vd_ref, qs_ref, ao_ref)


def _mixer(x2d, seq, layer, sinks, ln, win, poolw, pscale, wpb, qg, kg, rope_tab, mean_bd, wab, wout):
    tm = MIX_TM
    rows = MIX_PHASES * tm
    n_blk = tm // BLOCK
    n_steps = x2d.shape[0] // rows
    tiles_per_seq = seq // tm
    steps_per_seq = seq // rows
    block = pl.BlockSpec((rows, D_MODEL), lambda i: (i, 0))
    return pl.pallas_call(
        lambda *refs: _mixer_kernel(layer, tiles_per_seq, *refs),
        out_shape=jax.ShapeDtypeStruct(x2d.shape, F32),
        grid=(n_steps,),
        in_specs=[
            pl.BlockSpec(memory_space=pltpu.SMEM),
            block,
            pl.BlockSpec((rows, D_MODEL), lambda i: (jnp.minimum(i + 1, n_steps - 1), 0)),
            _layer(layer, (1, D_MODEL)),
            _layer(layer, (D_MODEL, IN_DIM)),
            _layer(layer, (2, MXU_DIM, MXU_DIM)),
            _layer(layer, (1, POOL_DIM)),
            _layer(layer, (POOL_DIM, D_MODEL)),
            _layer(layer, (1, MXU_DIM)),
            _layer(layer, (1, LANES)),
            pl.BlockSpec((rows, 3 * LANES), lambda i: (lax.rem(i, steps_per_seq), 0)),
            _resident((2 * MXU_DIM, MXU_DIM)),
            _layer(layer, (ATTN_DIM, D_MODEL)),
            _layer(layer, (D_MODEL, D_MODEL)),
        ],
        out_specs=block,
        scratch_shapes=[
            pltpu.VMEM((MIX_PHASES, tm, IN_DIM), F32),
            pltpu.VMEM((MIX_PHASES, tm, D_MODEL), BF16),
            pltpu.VMEM((POOL_WMAX + tm, POOL_DIM), F32),
            pltpu.VMEM((N_KV_HEADS, BLOCK + tm, LANES), BF16),
            pltpu.VMEM((N_KV_HEADS, BLOCK + tm, LANES), BF16),
            pltpu.VMEM((n_blk, N_KV_HEADS, GQA_GROUP * BLOCK, LANES), BF16),
            pltpu.VMEM((tm, ATTN_DIM), BF16),
        ],
        compiler_params=pltpu.CompilerParams(
            dimension_semantics=("arbitrary",), vmem_limit_bytes=VMEM_LIMIT_BYTES),
        name="gated_token_mixers",
    )(sinks, x2d, x2d, ln, win, poolw, pscale, wpb, qg, kg, rope_tab, mean_bd, wab, wout)


def _rope_table(seq):
    half = ROT_DIM // 2
    pos = jnp.arange(seq, dtype=F32)
    inv_freq = ROPE_THETA ** (-jnp.arange(0, ROT_DIM, 2, dtype=F32) / ROT_DIM)
    ang = pos[:, None] * inv_freq[None, :]
    cos, sin = jnp.cos(ang), jnp.sin(ang)
    pad = HEAD_DIM - ROT_DIM
    c64 = jnp.concatenate([cos, cos, jnp.ones((seq, pad), F32)], axis=1)
    s_hi = jnp.concatenate([-sin, jnp.zeros((seq, half + pad), F32)], axis=1)
    s_lo = jnp.concatenate([jnp.zeros((seq, half), F32), sin, jnp.zeros((seq, pad), F32)], axis=1)
    return jnp.concatenate([c64, c64, s_hi, s_hi, s_lo, s_lo], axis=1)


def _block_diag_pairs(pool_w):
    depth = pool_w.shape[0]
    out = jnp.zeros((depth, 2, MXU_DIM, MXU_DIM), pool_w.dtype)
    out = out.at[:, :, :LANES, :LANES].set(pool_w[:, 0::2])
    return out.at[:, :, LANES:, LANES:].set(pool_w[:, 1::2])


def kernel(x, ln_ffn1, w_ffn1_gu, w_ffn1_down, ln_mix, w_in, pool_w, pool_scale, w_pool_branch,
           q_norm, k_norm, sinks, w_attn_branch, w_out, ln_ffn2, w_ffn2_gu, w_ffn2_down):
    bsz, seq, _ = x.shape
    depth = ln_ffn1.shape[0]
    assert seq % (MIX_PHASES * MIX_TM) == 0 and (bsz * seq) % FFN_TM == 0

    rope_tab = _rope_table(seq)
    head_id = (jnp.arange(2 * MXU_DIM) % MXU_DIM) // HEAD_DIM
    mean_bd = ((head_id[:, None] == head_id[None, :MXU_DIM]) * (1.0 / HEAD_DIM)).astype(BF16)

    wgu1, wd1 = w_ffn1_gu.astype(BF16), w_ffn1_down.astype(BF16)
    wgu2, wd2 = w_ffn2_gu.astype(BF16), w_ffn2_down.astype(BF16)
    win, wpb = w_in.astype(BF16), w_pool_branch.astype(BF16)
    wab, wout = w_attn_branch.astype(BF16), w_out.astype(BF16)
    poolw = _block_diag_pairs(pool_w).astype(BF16)
    ln1, lnm, ln2 = ln_ffn1[:, None, :], ln_mix[:, None, :], ln_ffn2[:, None, :]
    pscale = pool_scale[:, None, :]
    qg = jnp.tile(q_norm * (HEAD_DIM ** -0.5), (1, MXU_DIM // HEAD_DIM))[:, None, :]
    kg = jnp.tile(k_norm, (1, LANES // HEAD_DIM))[:, None, :]

    x2d = x.reshape(bsz * seq, D_MODEL)
    for l in range(depth):
        x2d = _ffn(x2d, l, ln1, wgu1, wd1)
        x2d = _mixer(x2d, seq, l, sinks, lnm, win, poolw, pscale, wpb,
                     qg, kg, rope_tab, mean_bd, wab, wout)
        x2d = _ffn(x2d, l, ln2, wgu2, wd2)
    return x2d.reshape(bsz, seq, D_MODEL)
```

```python
import jax
import jax.numpy as jnp
from jax import lax
from jax.experimental import pallas as pl
from jax.experimental.pallas import tpu as pltpu

D_MODEL = 1024
D_FF = 2816
POOL_WINDOWS = (2, 4, 8, 16)
POOL_WMAX = max(POOL_WINDOWS)
POOL_DIM = 512
HEAD_DIM = 64
N_Q_HEADS = 8
N_KV_HEADS = 2
GQA_GROUP = N_Q_HEADS // N_KV_HEADS
ATTN_DIM = N_Q_HEADS * HEAD_DIM
KV_DIM = N_KV_HEADS * HEAD_DIM
IN_DIM = POOL_DIM + ATTN_DIM + 2 * KV_DIM + 2 * D_MODEL
IN_SPLITS = (POOL_DIM, POOL_DIM + ATTN_DIM, POOL_DIM + ATTN_DIM + 2 * KV_DIM)
BLOCK = 128
ROPE_THETA = 500000.0
ROT_DIM = HEAD_DIM // 4
EPS = 1e-6

LANES = 128
MXU_DIM = 256
VMEM_LIMIT_BYTES = 62 * 1024 * 1024

FFN_TM = 1024
FFN_FC = MXU_DIM
MIX_TM = 512
MIX_PHASES = 2
IN_CHUNK = MXU_DIM

F32 = jnp.float32
BF16 = jnp.bfloat16


def _dot(a, b):
    return jnp.dot(a, b, preferred_element_type=F32)


def _rms_rows(x, gain):
    ms = jnp.mean(x * x, axis=-1, keepdims=True)
    return x * lax.rsqrt(ms + EPS) * gain


def _sigmoid(x):
    return 1.0 / (1.0 + jnp.exp(-x))


def _resident(shape):
    return pl.BlockSpec(shape, lambda *_: (0,) * len(shape), pipeline_mode=pl.Buffered(1))


def _layer(layer, shape):
    return pl.BlockSpec((None,) + shape, lambda *_: (layer,) + (0,) * len(shape),
                        pipeline_mode=pl.Buffered(1))


def _ffn_kernel(x_ref, ln_ref, wgu_ref, wd_ref, o_ref, act_ref):
    x = x_ref[...]
    h = _rms_rows(x, ln_ref[...]).astype(BF16)
    for c in range(D_FF // FFN_FC):
        cols = slice(c * FFN_FC, (c + 1) * FFN_FC)
        g = _dot(h, wgu_ref[:, cols])
        u = _dot(h, wgu_ref[:, D_FF + c * FFN_FC:D_FF + (c + 1) * FFN_FC])
        act_ref[:, cols] = (g * _sigmoid(g) * u).astype(BF16)
    y = _dot(act_ref[...], wd_ref[...])
    o_ref[...] = x + 0.5 * y


def _ffn(x2d, layer, ln, wgu, wd):
    n_tok = x2d.shape[0]
    return pl.pallas_call(
        _ffn_kernel,
        out_shape=jax.ShapeDtypeStruct(x2d.shape, F32),
        grid=(n_tok // FFN_TM,),
        in_specs=[
            pl.BlockSpec((FFN_TM, D_MODEL), lambda i: (i, 0)),
            _layer(layer, (1, D_MODEL)),
            _layer(layer, (D_MODEL, 2 * D_FF)),
            _layer(layer, (D_FF, D_MODEL)),
        ],
        out_specs=pl.BlockSpec((FFN_TM, D_MODEL), lambda i: (i, 0)),
        scratch_shapes=[pltpu.VMEM((FFN_TM, D_FF), BF16)],
        compiler_params=pltpu.CompilerParams(
            dimension_semantics=("arbitrary",), vmem_limit_bytes=VMEM_LIMIT_BYTES),
        name="swiglu_halfstep",
    )(x2d, ln, wgu, wd)


def _head_rms(xq, gain, mean_blockdiag):
    x2 = xq * xq
    hi = x2.astype(BF16)
    lo = (x2 - hi.astype(F32)).astype(BF16)
    ms = _dot(jnp.concatenate([hi, lo], axis=1), mean_blockdiag)
    return xq * lax.rsqrt(ms + EPS) * gain


def _mix_phase(layer, t, at_seq_start, x_rows, rope, o_ref, rows_out, z_ref, h_ref, hnext_ref,
               znext_ref, normalise_ahead, sinks_ref, win_ref, poolw_ref, pscale_ref, wpb_ref,
               qg_ref, kg_ref, mean_ref, wab_ref, wout_ref, uext_ref, kd_ref, vd_ref, qs_ref, ao_ref):
    tm = MIX_TM
    n_blk = tm // BLOCK
    c_q, c_kv, c_gate = IN_SPLITS
    chunks = iter(range(c_gate // IN_CHUNK))
    gate_chunks = iter(range(c_gate // IN_CHUNK, IN_DIM // IN_CHUNK))

    def project_next(n):
        for _ in range(n):
            c = next(chunks)
            cols = slice(c * IN_CHUNK, (c + 1) * IN_CHUNK)
            znext_ref[:, cols] = _dot(hnext_ref[...], win_ref[:, cols])

    def gate_chunk():
        c = next(gate_chunks)
        return _sigmoid(_dot(h_ref[...], win_ref[:, c * IN_CHUNK:(c + 1) * IN_CHUNK]))

    rope_c = rope[:, 0:LANES]
    rope_s_hi = rope[:, LANES:2 * LANES]
    rope_s_lo = rope[:, 2 * LANES:3 * LANES]

    def apply_rope(xc):
        return (xc * rope_c + pltpu.roll(xc, LANES - ROT_DIM // 2, 1) * rope_s_hi
                + pltpu.roll(xc, ROT_DIM // 2, 1) * rope_s_lo)

    lane = lax.broadcasted_iota(jnp.int32, (1, LANES), 1)
    low_half = lane < HEAD_DIM

    gate_pool, gate_attn = [], []
    u = z_ref[:, 0:c_q]
    uext_ref[POOL_WMAX:POOL_WMAX + tm, :] = u
    tok = t * tm + lax.broadcasted_iota(jnp.int32, (tm, 1), 0)
    d_groups = []
    for g, w in enumerate(POOL_WINDOWS):
        gate_pool.append(gate_chunk())
        lanes = slice(g * LANES, (g + 1) * LANES)
        win = uext_ref[:, lanes]
        span = 1
        while span < w:
            win = win + pltpu.roll(win, span, 0)
            span *= 2
        inv_count = 1.0 / jnp.minimum(tok + 1, w).astype(F32)
        d_groups.append(win[POOL_WMAX:] * inv_count - u[:, lanes])
    uext_ref[0:POOL_WMAX, :] = uext_ref[tm:tm + POOL_WMAX, :]
    y_pairs = []
    for pair in range(2):
        d_pair = jnp.concatenate(d_groups[2 * pair:2 * pair + 2], axis=1).astype(BF16)
        y_pairs.append(_dot(d_pair, poolw_ref[pair]))
    y = jnp.concatenate(y_pairs, axis=1) * pscale_ref[...]
    gated_pool = jnp.concatenate(gate_pool, axis=1) * _dot(y.astype(BF16), wpb_ref[...])

    k = z_ref[:, c_kv:c_kv + KV_DIM]
    v = z_ref[:, c_kv + KV_DIM:c_gate]
    mean_k = mean_ref[0:LANES, 0:LANES]
    kn = apply_rope(_head_rms(k, kg_ref[...], jnp.concatenate([mean_k, mean_k], axis=0)))
    kn_sw = pltpu.roll(kn, HEAD_DIM, 1)
    v_sw = pltpu.roll(v, HEAD_DIM, 1)
    rows_new = slice(BLOCK, BLOCK + tm)
    kd_ref[0, rows_new, :] = jnp.where(low_half, kn, kn_sw).astype(BF16)
    kd_ref[1, rows_new, :] = jnp.where(low_half, kn_sw, kn).astype(BF16)
    vd_ref[0, rows_new, :] = jnp.where(low_half, v, v_sw).astype(BF16)
    vd_ref[1, rows_new, :] = jnp.where(low_half, v_sw, v).astype(BF16)

    for half in range(2):
        qn = _head_rms(z_ref[:, c_q + half * MXU_DIM:c_q + (half + 1) * MXU_DIM],
                       qg_ref[...], mean_ref[...])
        for sub in range(2):
            qc = apply_rope(qn[:, sub * LANES:(sub + 1) * LANES])
            q_lo = jnp.where(low_half, qc, 0.0).astype(BF16)
            q_hi = jnp.where(low_half, 0.0, qc).astype(BF16)
            for b in range(n_blk):
                rows = slice(b * BLOCK, (b + 1) * BLOCK)
                qs_ref[b, half, (2 * sub) * BLOCK:(2 * sub + 1) * BLOCK, :] = q_lo[rows]
                qs_ref[b, half, (2 * sub + 1) * BLOCK:(2 * sub + 2) * BLOCK, :] = q_hi[rows]
        project_next(1)

    g_rows = GQA_GROUP * BLOCK
    qi = lax.broadcasted_iota(jnp.int32, (g_rows, 2 * BLOCK), 0) & (BLOCK - 1)
    ki = lax.broadcasted_iota(jnp.int32, (g_rows, 2 * BLOCK), 1)
    band = (ki > qi) & (ki <= qi + BLOCK)
    head_of_row = lax.broadcasted_iota(jnp.int32, (g_rows, 1), 0) // BLOCK
    fillers = iter(("next", "gate", "next", "gate", "next", "gate", "gate"))
    for b in range(n_blk):
        keys = slice(b * BLOCK, (b + 2) * BLOCK)
        if b == 0 and at_seq_start:
            mask = band & (ki >= jnp.where(t > 0, 0, BLOCK))
        else:
            mask = band
        for hk in range(N_KV_HEADS):
            sink = jnp.zeros((g_rows, 1), F32)
            for j in range(GQA_GROUP):
                sink = jnp.where(head_of_row == j, sinks_ref[layer, hk * GQA_GROUP + j], sink)
            s = lax.dot_general(qs_ref[b, hk], kd_ref[hk, keys, :],
                                (((1,), (1,)), ((), ())), preferred_element_type=F32)
            s = jnp.where(mask, s, -jnp.inf)
            m = jnp.maximum(jnp.max(s, axis=-1, keepdims=True), sink)
            p = jnp.exp(s - m)
            denom = jnp.sum(p, axis=-1, keepdims=True) + jnp.exp(sink - m)
            o = _dot(p.astype(BF16), vd_ref[hk, keys, :]) / denom
            rows = slice(b * BLOCK, (b + 1) * BLOCK)
            for sub in range(2):
                pair = jnp.where(low_half, o[(2 * sub) * BLOCK:(2 * sub + 1) * BLOCK],
                                 o[(2 * sub + 1) * BLOCK:(2 * sub + 2) * BLOCK])
                col = (2 * hk + sub) * LANES
                ao_ref[rows, col:col + LANES] = pair.astype(BF16)
            filler = next(fillers, None)
            if filler == "gate":
                gate_attn.append(gate_chunk())
            elif filler == "next":
                project_next(1)
    kd_ref[:, 0:BLOCK, :] = kd_ref[:, tm:tm + BLOCK, :]
    vd_ref[:, 0:BLOCK, :] = vd_ref[:, tm:tm + BLOCK, :]
    b_attn = _dot(ao_ref[...], wab_ref[...])

    merged = gated_pool + jnp.concatenate(gate_attn, axis=1) * b_attn
    assert next(chunks, None) is None and next(gate_chunks, None) is None
    normalise_ahead()
    o_ref[rows_out, :] = x_rows[...] + _dot(merged.astype(BF16), wout_ref[...])


def _mixer_kernel(layer, tiles_per_seq, sinks_ref, x_ref, xnext_ref, ln_ref, win_ref, poolw_ref,
                  pscale_ref, wpb_ref, qg_ref, kg_ref, rope_ref, mean_ref, wab_ref, wout_ref,
                  o_ref, z_ref, h_ref, uext_ref, kd_ref, vd_ref, qs_ref, ao_ref):
    tm = MIX_TM
    i = pl.program_id(0)
    t0 = lax.rem(i * MIX_PHASES, tiles_per_seq)

    def norm_rows(src_ref, ph):
        rows = slice(ph * tm, (ph + 1) * tm)
        h_ref[ph] = _rms_rows(src_ref[rows, :], ln_ref[...]).astype(BF16)

    @pl.when(t0 == 0)
    def _():
        uext_ref[0:POOL_WMAX, :] = jnp.zeros((POOL_WMAX, POOL_DIM), F32)
        kd_ref[:, 0:BLOCK, :] = jnp.zeros((N_KV_HEADS, BLOCK, LANES), BF16)
        vd_ref[:, 0:BLOCK, :] = jnp.zeros((N_KV_HEADS, BLOCK, LANES), BF16)

    @pl.when(i == 0)
    def _():
        norm_rows(x_ref, 0)
        for c in range(IN_SPLITS[-1] // IN_CHUNK):
            cols = slice(c * IN_CHUNK, (c + 1) * IN_CHUNK)
            z_ref[0, :, cols] = _dot(h_ref[0], win_ref[:, cols])
        norm_rows(x_ref, 1)

    for ph in range(MIX_PHASES):
        nxt = (ph + 1) % MIX_PHASES
        rows = slice(ph * tm, (ph + 1) * tm)
        _mix_phase(layer, t0 + ph, ph == 0, x_ref.at[rows], rope_ref.at[rows], o_ref, rows,
                   z_ref.at[ph], h_ref.at[ph], h_ref.at[nxt], z_ref.at[nxt],
                   lambda ph=ph: norm_rows(xnext_ref, ph),
                   sinks_ref, win_ref, poolw_ref, pscale_ref, wpb_ref, qg_ref, kg_ref, mean_ref,
                   wab_ref, wout_ref, uext_ref, kd_ref, vd_ref, qs_ref, ao_ref)


def _mixer(x2d, seq, layer, sinks, ln, win, poolw, pscale, wpb, qg, kg, rope_tab, mean_bd, wab, wout):
    tm = MIX_TM
    rows = MIX_PHASES * tm
    n_blk = tm // BLOCK
    n_steps = x2d.shape[0] // rows
    tiles_per_seq = seq // tm
    steps_per_seq = seq // rows
    block = pl.BlockSpec((rows, D_MODEL), lambda i: (i, 0))
    return pl.pallas_call(
        lambda *refs: _mixer_kernel(layer, tiles_per_seq, *refs),
        out_shape=jax.ShapeDtypeStruct(x2d.shape, F32),
        grid=(n_steps,),
        in_specs=[
            pl.BlockSpec(memory_space=pltpu.SMEM),
            block,
            pl.BlockSpec((rows, D_MODEL), lambda i: (jnp.minimum(i + 1, n_steps - 1), 0)),
            _layer(layer, (1, D_MODEL)),
            _layer(layer, (D_MODEL, IN_DIM)),
            _layer(layer, (2, MXU_DIM, MXU_DIM)),
            _layer(layer, (1, POOL_DIM)),
            _layer(layer, (POOL_DIM, D_MODEL)),
            _layer(layer, (1, MXU_DIM)),
            _layer(layer, (1, LANES)),
            pl.BlockSpec((rows, 3 * LANES), lambda i: (lax.rem(i, steps_per_seq), 0)),
            _resident((2 * MXU_DIM, MXU_DIM)),
            _layer(layer, (ATTN_DIM, D_MODEL)),
            _layer(layer, (D_MODEL, D_MODEL)),
        ],
        out_specs=block,
        scratch_shapes=[
            pltpu.VMEM((MIX_PHASES, tm, IN_SPLITS[-1]), F32),
            pltpu.VMEM((MIX_PHASES, tm, D_MODEL), BF16),
            pltpu.VMEM((POOL_WMAX + tm, POOL_DIM), F32),
            pltpu.VMEM((N_KV_HEADS, BLOCK + tm, LANES), BF16),
            pltpu.VMEM((N_KV_HEADS, BLOCK + tm, LANES), BF16),
            pltpu.VMEM((n_blk, N_KV_HEADS, GQA_GROUP * BLOCK, LANES), BF16),
            pltpu.VMEM((tm, ATTN_DIM), BF16),
        ],
        compiler_params=pltpu.CompilerParams(
            dimension_semantics=("arbitrary",), vmem_limit_bytes=VMEM_LIMIT_BYTES),
        name="gated_token_mixers",
    )(sinks, x2d, x2d, ln, win, poolw, pscale, wpb, qg, kg, rope_tab, mean_bd, wab, wout)


def _rope_table(seq):
    half = ROT_DIM // 2
    pos = jnp.arange(seq, dtype=F32)
    inv_freq = ROPE_THETA ** (-jnp.arange(0, ROT_DIM, 2, dtype=F32) / ROT_DIM)
    dim = jnp.arange(LANES) % HEAD_DIM
    ang = pos[:, None] * inv_freq[dim % half][None, :]
    cos, sin = jnp.cos(ang), jnp.sin(ang)
    c = jnp.where(dim < ROT_DIM, cos, 1.0)
    s_hi = jnp.where(dim < half, -sin, 0.0)
    s_lo = jnp.where((dim >= half) & (dim < ROT_DIM), sin, 0.0)
    return jnp.concatenate([c, s_hi, s_lo], axis=1)


def _block_diag_pairs(pool_w):
    depth = pool_w.shape[0]
    out = jnp.zeros((depth, 2, MXU_DIM, MXU_DIM), pool_w.dtype)
    out = out.at[:, :, :LANES, :LANES].set(pool_w[:, 0::2])
    return out.at[:, :, LANES:, LANES:].set(pool_w[:, 1::2])


def kernel(x, ln_ffn1, w_ffn1_gu, w_ffn1_down, ln_mix, w_in, pool_w, pool_scale, w_pool_branch,
           q_norm, k_norm, sinks, w_attn_branch, w_out, ln_ffn2, w_ffn2_gu, w_ffn2_down):
    bsz, seq, _ = x.shape
    depth = ln_ffn1.shape[0]
    assert seq % (MIX_PHASES * MIX_TM) == 0 and (bsz * seq) % FFN_TM == 0

    rope_tab = _rope_table(seq)
    head_id = (jnp.arange(2 * MXU_DIM) % MXU_DIM) // HEAD_DIM
    mean_bd = ((head_id[:, None] == head_id[None, :MXU_DIM]) * (1.0 / HEAD_DIM)).astype(BF16)

    wgu1, wd1 = w_ffn1_gu.astype(BF16), w_ffn1_down.astype(BF16)
    wgu2, wd2 = w_ffn2_gu.astype(BF16), w_ffn2_down.astype(BF16)
    win, wpb = w_in.astype(BF16), w_pool_branch.astype(BF16)
    wab, wout = w_attn_branch.astype(BF16), w_out.astype(BF16)
    poolw = _block_diag_pairs(pool_w).astype(BF16)
    ln1, lnm, ln2 = ln_ffn1[:, None, :], ln_mix[:, None, :], ln_ffn2[:, None, :]
    pscale = pool_scale[:, None, :]
    qg = jnp.tile(q_norm * (HEAD_DIM ** -0.5), (1, MXU_DIM // HEAD_DIM))[:, None, :]
    kg = jnp.tile(k_norm, (1, LANES // HEAD_DIM))[:, None, :]

    x2d = x.reshape(bsz * seq, D_MODEL)
    for l in range(depth):
        x2d = _ffn(x2d, l, ln1, wgu1, wd1)
        x2d = _mixer(x2d, seq, l, sinks, lnm, win, poolw, pscale, wpb,
                     qg, kg, rope_tab, mean_bd, wab, wout)
        x2d = _ffn(x2d, l, ln2, wgu2, wd2)
    return x2d.reshape(bsz, seq, D_MODEL)
```

```python
import jax
import jax.numpy as jnp
from jax import lax
from jax.experimental import pallas as pl
from jax.experimental.pallas import tpu as pltpu

D_MODEL = 1024
D_FF = 2816
POOL_WINDOWS = (2, 4, 8, 16)
POOL_WMAX = max(POOL_WINDOWS)
POOL_DIM = 512
HEAD_DIM = 64
N_Q_HEADS = 8
N_KV_HEADS = 2
GQA_GROUP = N_Q_HEADS // N_KV_HEADS
ATTN_DIM = N_Q_HEADS * HEAD_DIM
KV_DIM = N_KV_HEADS * HEAD_DIM
IN_DIM = POOL_DIM + ATTN_DIM + 2 * KV_DIM + 2 * D_MODEL
IN_SPLITS = (POOL_DIM, POOL_DIM + ATTN_DIM, POOL_DIM + ATTN_DIM + 2 * KV_DIM)
BLOCK = 128
ROPE_THETA = 500000.0
ROT_DIM = HEAD_DIM // 4
EPS = 1e-6
LOG2_E = 1.4426950408889634

LANES = 128
MXU_DIM = 256
VMEM_LIMIT_BYTES = 62 * 1024 * 1024

FFN_TM = 1024
FFN_FC = MXU_DIM
MIX_TM = 512
MIX_PHASES = 2
IN_CHUNK = MXU_DIM

F32 = jnp.float32
BF16 = jnp.bfloat16


def _dot(a, b):
    return jnp.dot(a, b, preferred_element_type=F32)


def _rms_rows(x, gain):
    ms = jnp.mean(x * x, axis=-1, keepdims=True)
    return x * lax.rsqrt(ms + EPS) * gain


def _sigmoid(x):
    return 1.0 / (1.0 + jnp.exp(-x))


def _resident(shape):
    return pl.BlockSpec(shape, lambda *_: (0,) * len(shape), pipeline_mode=pl.Buffered(1))


def _layer(layer, shape):
    return pl.BlockSpec((None,) + shape, lambda *_: (layer,) + (0,) * len(shape),
                        pipeline_mode=pl.Buffered(1))


def _ffn_kernel(x_ref, ln_ref, wgu_ref, wd_ref, o_ref, act_ref):
    x = x_ref[...]
    h = _rms_rows(x, ln_ref[...]).astype(BF16)
    for c in range(D_FF // FFN_FC):
        cols = slice(c * FFN_FC, (c + 1) * FFN_FC)
        g = _dot(h, wgu_ref[:, cols])
        u = _dot(h, wgu_ref[:, D_FF + c * FFN_FC:D_FF + (c + 1) * FFN_FC])
        act_ref[:, cols] = (g * _sigmoid(g) * u).astype(BF16)
    y = _dot(act_ref[...], wd_ref[...])
    o_ref[...] = x + 0.5 * y


def _ffn(x2d, layer, ln, wgu, wd):
    n_tok = x2d.shape[0]
    return pl.pallas_call(
        _ffn_kernel,
        out_shape=jax.ShapeDtypeStruct(x2d.shape, F32),
        grid=(n_tok // FFN_TM,),
        in_specs=[
            pl.BlockSpec((FFN_TM, D_MODEL), lambda i: (i, 0)),
            _layer(layer, (1, D_MODEL)),
            _layer(layer, (D_MODEL, 2 * D_FF)),
            _layer(layer, (D_FF, D_MODEL)),
        ],
        out_specs=pl.BlockSpec((FFN_TM, D_MODEL), lambda i: (i, 0)),
        scratch_shapes=[pltpu.VMEM((FFN_TM, D_FF), BF16)],
        compiler_params=pltpu.CompilerParams(
            dimension_semantics=("arbitrary",), vmem_limit_bytes=VMEM_LIMIT_BYTES),
        name="swiglu_halfstep",
    )(x2d, ln, wgu, wd)


def _head_rms(xq, gain, mean_blockdiag):
    x2 = xq * xq
    hi = x2.astype(BF16)
    lo = (x2 - hi.astype(F32)).astype(BF16)
    ms = _dot(jnp.concatenate([hi, lo], axis=1), mean_blockdiag)
    return xq * lax.rsqrt(ms + EPS) * gain


def _mix_phase(layer, t, at_seq_start, x_rows, rope, o_ref, rows_out, z_ref, h_ref, hnext_ref,
               znext_ref, normalise_ahead, sinks_ref, win_ref, poolw_ref, pscale_ref, wpb_ref,
               qg_ref, kg_ref, mean_ref, wab_ref, wout_ref, uext_ref, kd_ref, vd_ref, qs_ref, ao_ref):
    tm = MIX_TM
    n_blk = tm // BLOCK
    c_q, c_kv, c_gate = IN_SPLITS
    chunks = iter(range(c_gate // IN_CHUNK))
    gate_chunks = iter(range(c_gate // IN_CHUNK, IN_DIM // IN_CHUNK))

    def project_next(n):
        for _ in range(n):
            c = next(chunks)
            cols = slice(c * IN_CHUNK, (c + 1) * IN_CHUNK)
            znext_ref[:, cols] = _dot(hnext_ref[...], win_ref[:, cols])

    def gate_chunk():
        c = next(gate_chunks)
        return _sigmoid(_dot(h_ref[...], win_ref[:, c * IN_CHUNK:(c + 1) * IN_CHUNK]))

    rope_c = rope[:, 0:LANES]
    rope_s_hi = rope[:, LANES:2 * LANES]
    rope_s_lo = rope[:, 2 * LANES:3 * LANES]

    def apply_rope(xc):
        return (xc * rope_c + pltpu.roll(xc, LANES - ROT_DIM // 2, 1) * rope_s_hi
                + pltpu.roll(xc, ROT_DIM // 2, 1) * rope_s_lo)

    lane = lax.broadcasted_iota(jnp.int32, (1, LANES), 1)
    low_half = lane < HEAD_DIM

    gate_pool, gate_attn = [], []
    u = z_ref[:, 0:c_q]
    uext_ref[POOL_WMAX:POOL_WMAX + tm, :] = u
    tok = t * tm + lax.broadcasted_iota(jnp.int32, (tm, 1), 0)
    inv_tok = 1.0 / (tok + 1).astype(F32)
    d_groups = []
    for g, w in enumerate(POOL_WINDOWS):
        gate_pool.append(gate_chunk())
        lanes = slice(g * LANES, (g + 1) * LANES)
        win = uext_ref[:, lanes]
        span = 1
        while span < w:
            win = win + pltpu.roll(win, span, 0)
            span *= 2
        inv_count = jnp.where(tok + 1 < w, inv_tok, 1.0 / w)
        d_groups.append(win[POOL_WMAX:] * inv_count - u[:, lanes])
    uext_ref[0:POOL_WMAX, :] = uext_ref[tm:tm + POOL_WMAX, :]
    y_pairs = []
    for pair in range(2):
        d_pair = jnp.concatenate(d_groups[2 * pair:2 * pair + 2], axis=1).astype(BF16)
        y_pairs.append(_dot(d_pair, poolw_ref[pair]))
    y = jnp.concatenate(y_pairs, axis=1) * pscale_ref[...]
    gated_pool = jnp.concatenate(gate_pool, axis=1) * _dot(y.astype(BF16), wpb_ref[...])

    k = z_ref[:, c_kv:c_kv + KV_DIM]
    v = z_ref[:, c_kv + KV_DIM:c_gate]
    mean_k = mean_ref[0:LANES, 0:LANES]
    kn = apply_rope(_head_rms(k, kg_ref[...], jnp.concatenate([mean_k, mean_k], axis=0)))
    kn_sw = pltpu.roll(kn, HEAD_DIM, 1)
    v_sw = pltpu.roll(v, HEAD_DIM, 1)
    rows_new = slice(BLOCK, BLOCK + tm)
    kd_ref[0, rows_new, :] = jnp.where(low_half, kn, 0.0).astype(BF16)
    kd_ref[1, rows_new, :] = jnp.where(low_half, 0.0, kn_sw).astype(BF16)
    kd_ref[2, rows_new, :] = jnp.where(low_half, kn_sw, 0.0).astype(BF16)
    kd_ref[3, rows_new, :] = jnp.where(low_half, 0.0, kn).astype(BF16)
    vd_ref[0, rows_new, :] = jnp.where(low_half, v, v_sw).astype(BF16)
    vd_ref[1, rows_new, :] = jnp.where(low_half, v_sw, v).astype(BF16)

    for half in range(2):
        qn = _head_rms(z_ref[:, c_q + half * MXU_DIM:c_q + (half + 1) * MXU_DIM],
                       qg_ref[...], mean_ref[...])
        for sub in range(2):
            qc = apply_rope(qn[:, sub * LANES:(sub + 1) * LANES]).astype(BF16)
            for b in range(n_blk):
                qs_ref[b, half, sub * BLOCK:(sub + 1) * BLOCK, :] = qc[b * BLOCK:(b + 1) * BLOCK]
        project_next(1)

    g_rows = GQA_GROUP * BLOCK
    qi = lax.broadcasted_iota(jnp.int32, (g_rows, 2 * BLOCK), 0) & (BLOCK - 1)
    ki = lax.broadcasted_iota(jnp.int32, (g_rows, 2 * BLOCK), 1)
    band = (ki > qi) & (ki <= qi + BLOCK)
    head_of_row = lax.broadcasted_iota(jnp.int32, (g_rows, 1), 0) // BLOCK
    fillers = iter(("next", "gate", "next", "gate", "next", "gate", "gate"))
    for b in range(n_blk):
        keys = slice(b * BLOCK, (b + 2) * BLOCK)
        if b == 0 and at_seq_start:
            mask = band & (ki >= jnp.where(t > 0, 0, BLOCK))
        else:
            mask = band
        for hk in range(N_KV_HEADS):
            sink = jnp.zeros((g_rows, 1), F32)
            for j, head in enumerate((0, 2, 1, 3)):
                sink = jnp.where(head_of_row == j, sinks_ref[layer, hk * GQA_GROUP + head], sink)
            s = jnp.concatenate(
                [lax.dot_general(qs_ref[b, hk], kd_ref[2 * hk + hi, keys, :],
                                 (((1,), (1,)), ((), ())), preferred_element_type=F32)
                 for hi in range(2)], axis=0)
            s = jnp.where(mask, s, -jnp.inf)
            m = jnp.maximum(jnp.max(s, axis=-1, keepdims=True), sink)
            p = jnp.exp2(s - m)
            denom = jnp.sum(p, axis=-1, keepdims=True) + jnp.exp2(sink - m)
            o = _dot(p.astype(BF16), vd_ref[hk, keys, :]) / denom
            rows = slice(b * BLOCK, (b + 1) * BLOCK)
            for sub in range(2):
                pair = jnp.where(low_half, o[sub * BLOCK:(sub + 1) * BLOCK],
                                 o[(2 + sub) * BLOCK:(3 + sub) * BLOCK])
                col = (2 * hk + sub) * LANES
                ao_ref[rows, col:col + LANES] = pair.astype(BF16)
            filler = next(fillers, None)
            if filler == "gate":
                gate_attn.append(gate_chunk())
            elif filler == "next":
                project_next(1)
    kd_ref[:, 0:BLOCK, :] = kd_ref[:, tm:tm + BLOCK, :]
    vd_ref[:, 0:BLOCK, :] = vd_ref[:, tm:tm + BLOCK, :]
    b_attn = _dot(ao_ref[...], wab_ref[...])

    merged = gated_pool + jnp.concatenate(gate_attn, axis=1) * b_attn
    assert next(chunks, None) is None and next(gate_chunks, None) is None
    normalise_ahead()
    o_ref[rows_out, :] = x_rows[...] + _dot(merged.astype(BF16), wout_ref[...])


def _mixer_kernel(layer, tiles_per_seq, sinks_ref, x_ref, xnext_ref, ln_ref, win_ref, poolw_ref,
                  pscale_ref, wpb_ref, qg_ref, kg_ref, rope_ref, mean_ref, wab_ref, wout_ref,
                  o_ref, z_ref, h_ref, uext_ref, kd_ref, vd_ref, qs_ref, ao_ref):
    tm = MIX_TM
    i = pl.program_id(0)
    t0 = lax.rem(i * MIX_PHASES, tiles_per_seq)

    def norm_rows(src_ref, ph):
        rows = slice(ph * tm, (ph + 1) * tm)
        h_ref[ph] = _rms_rows(src_ref[rows, :], ln_ref[...]).astype(BF16)

    @pl.when(t0 == 0)
    def _():
        uext_ref[0:POOL_WMAX, :] = jnp.zeros((POOL_WMAX, POOL_DIM), F32)
        kd_ref[:, 0:BLOCK, :] = jnp.zeros((2 * N_KV_HEADS, BLOCK, LANES), BF16)
        vd_ref[:, 0:BLOCK, :] = jnp.zeros((N_KV_HEADS, BLOCK, LANES), BF16)

    @pl.when(i == 0)
    def _():
        norm_rows(x_ref, 0)
        for c in range(IN_SPLITS[-1] // IN_CHUNK):
            cols = slice(c * IN_CHUNK, (c + 1) * IN_CHUNK)
            z_ref[0, :, cols] = _dot(h_ref[0], win_ref[:, cols])
        norm_rows(x_ref, 1)

    for ph in range(MIX_PHASES):
        nxt = (ph + 1) % MIX_PHASES
        rows = slice(ph * tm, (ph + 1) * tm)
        _mix_phase(layer, t0 + ph, ph == 0, x_ref.at[rows], rope_ref.at[rows], o_ref, rows,
                   z_ref.at[ph], h_ref.at[ph], h_ref.at[nxt], z_ref.at[nxt],
                   lambda ph=ph: norm_rows(xnext_ref, ph),
                   sinks_ref, win_ref, poolw_ref, pscale_ref, wpb_ref, qg_ref, kg_ref, mean_ref,
                   wab_ref, wout_ref, uext_ref, kd_ref, vd_ref, qs_ref, ao_ref)


def _mixer(x2d, seq, layer, sinks, ln, win, poolw, pscale, wpb, qg, kg, rope_tab, mean_bd, wab, wout):
    tm = MIX_TM
    rows = MIX_PHASES * tm
    n_blk = tm // BLOCK
    n_steps = x2d.shape[0] // rows
    tiles_per_seq = seq // tm
    steps_per_seq = seq // rows
    block = pl.BlockSpec((rows, D_MODEL), lambda i: (i, 0))
    return pl.pallas_call(
        lambda *refs: _mixer_kernel(layer, tiles_per_seq, *refs),
        out_shape=jax.ShapeDtypeStruct(x2d.shape, F32),
        grid=(n_steps,),
        in_specs=[
            pl.BlockSpec(memory_space=pltpu.SMEM),
            block,
            pl.BlockSpec((rows, D_MODEL), lambda i: (jnp.minimum(i + 1, n_steps - 1), 0)),
            _layer(layer, (1, D_MODEL)),
            _layer(layer, (D_MODEL, IN_DIM)),
            _layer(layer, (2, MXU_DIM, MXU_DIM)),
            _layer(layer, (1, POOL_DIM)),
            _layer(layer, (POOL_DIM, D_MODEL)),
            _layer(layer, (1, MXU_DIM)),
            _layer(layer, (1, LANES)),
            pl.BlockSpec((rows, 3 * LANES), lambda i: (lax.rem(i, steps_per_seq), 0)),
            _resident((2 * MXU_DIM, MXU_DIM)),
            _layer(layer, (ATTN_DIM, D_MODEL)),
            _layer(layer, (D_MODEL, D_MODEL)),
        ],
        out_specs=block,
        scratch_shapes=[
            pltpu.VMEM((MIX_PHASES, tm, IN_SPLITS[-1]), F32),
            pltpu.VMEM((MIX_PHASES, tm, D_MODEL), BF16),
            pltpu.VMEM((POOL_WMAX + tm, POOL_DIM), F32),
            pltpu.VMEM((2 * N_KV_HEADS, BLOCK + tm, LANES), BF16),
            pltpu.VMEM((N_KV_HEADS, BLOCK + tm, LANES), BF16),
            pltpu.VMEM((n_blk, N_KV_HEADS, 2 * BLOCK, LANES), BF16),
            pltpu.VMEM((tm, ATTN_DIM), BF16),
        ],
        compiler_params=pltpu.CompilerParams(
            dimension_semantics=("arbitrary",), vmem_limit_bytes=VMEM_LIMIT_BYTES),
        name="gated_token_mixers",
    )(sinks, x2d, x2d, ln, win, poolw, pscale, wpb, qg, kg, rope_tab, mean_bd, wab, wout)


def _rope_table(seq):
    half = ROT_DIM // 2
    pos = jnp.arange(seq, dtype=F32)
    inv_freq = ROPE_THETA ** (-jnp.arange(0, ROT_DIM, 2, dtype=F32) / ROT_DIM)
    dim = jnp.arange(LANES) % HEAD_DIM
    ang = pos[:, None] * inv_freq[dim % half][None, :]
    cos, sin = jnp.cos(ang), jnp.sin(ang)
    c = jnp.where(dim < ROT_DIM, cos, 1.0)
    s_hi = jnp.where(dim < half, -sin, 0.0)
    s_lo = jnp.where((dim >= half) & (dim < ROT_DIM), sin, 0.0)
    return jnp.concatenate([c, s_hi, s_lo], axis=1)


def _block_diag_pairs(pool_w):
    depth = pool_w.shape[0]
    out = jnp.zeros((depth, 2, MXU_DIM, MXU_DIM), pool_w.dtype)
    out = out.at[:, :, :LANES, :LANES].set(pool_w[:, 0::2])
    return out.at[:, :, LANES:, LANES:].set(pool_w[:, 1::2])


def kernel(x, ln_ffn1, w_ffn1_gu, w_ffn1_down, ln_mix, w_in, pool_w, pool_scale, w_pool_branch,
           q_norm, k_norm, sinks, w_attn_branch, w_out, ln_ffn2, w_ffn2_gu, w_ffn2_down):
    bsz, seq, _ = x.shape
    depth = ln_ffn1.shape[0]
    assert seq % (MIX_PHASES * MIX_TM) == 0 and (bsz * seq) % FFN_TM == 0

    rope_tab = _rope_table(seq)
    head_id = (jnp.arange(2 * MXU_DIM) % MXU_DIM) // HEAD_DIM
    mean_bd = ((head_id[:, None] == head_id[None, :MXU_DIM]) * (1.0 / HEAD_DIM)).astype(BF16)

    wgu1, wd1 = w_ffn1_gu.astype(BF16), w_ffn1_down.astype(BF16)
    wgu2, wd2 = w_ffn2_gu.astype(BF16), w_ffn2_down.astype(BF16)
    win, wpb = w_in.astype(BF16), w_pool_branch.astype(BF16)
    wab, wout = w_attn_branch.astype(BF16), w_out.astype(BF16)
    poolw = _block_diag_pairs(pool_w).astype(BF16)
    ln1, lnm, ln2 = ln_ffn1[:, None, :], ln_mix[:, None, :], ln_ffn2[:, None, :]
    pscale = pool_scale[:, None, :]
    qg = jnp.tile(q_norm * (HEAD_DIM ** -0.5 * LOG2_E), (1, MXU_DIM // HEAD_DIM))[:, None, :]
    kg = jnp.tile(k_norm, (1, LANES // HEAD_DIM))[:, None, :]
    sinks = sinks * LOG2_E

    x2d = x.reshape(bsz * seq, D_MODEL)
    for l in range(depth):
        x2d = _ffn(x2d, l, ln1, wgu1, wd1)
        x2d = _mixer(x2d, seq, l, sinks, lnm, win, poolw, pscale, wpb,
                     qg, kg, rope_tab, mean_bd, wab, wout)
        x2d = _ffn(x2d, l, ln2, wgu2, wd2)
    return x2d.reshape(bsz, seq, D_MODEL)
```

```python
import jax
import jax.numpy as jnp
from jax import lax
from jax.experimental import pallas as pl
from jax.experimental.pallas import tpu as pltpu

D_MODEL = 1024
D_FF = 2816
POOL_WINDOWS = (2, 4, 8, 16)
POOL_WMAX = max(POOL_WINDOWS)
POOL_DIM = 512
HEAD_DIM = 64
N_Q_HEADS = 8
N_KV_HEADS = 2
GQA_GROUP = N_Q_HEADS // N_KV_HEADS
ATTN_DIM = N_Q_HEADS * HEAD_DIM
KV_DIM = N_KV_HEADS * HEAD_DIM
IN_DIM = POOL_DIM + ATTN_DIM + 2 * KV_DIM + 2 * D_MODEL
IN_SPLITS = (POOL_DIM, POOL_DIM + ATTN_DIM, POOL_DIM + ATTN_DIM + 2 * KV_DIM)
BLOCK = 128
ROPE_THETA = 500000.0
ROT_DIM = HEAD_DIM // 4
EPS = 1e-6
LOG2_E = 1.4426950408889634

LANES = 128
MXU_DIM = 256
VMEM_LIMIT_BYTES = 62 * 1024 * 1024

FFN_TM = 1024
FFN_FC = MXU_DIM
MIX_TM = 512
MIX_PHASES = 2
IN_CHUNK = MXU_DIM

F32 = jnp.float32
BF16 = jnp.bfloat16


def _dot(a, b):
    return jnp.dot(a, b, preferred_element_type=F32)


def _rms_rows(x, gain):
    ms = jnp.mean(x * x, axis=-1, keepdims=True)
    return x * lax.rsqrt(ms + EPS) * gain


def _sigmoid(x):
    return 1.0 / (1.0 + jnp.exp(-x))


def _resident(shape):
    return pl.BlockSpec(shape, lambda *_: (0,) * len(shape), pipeline_mode=pl.Buffered(1))


def _layer(layer, shape):
    return pl.BlockSpec((None,) + shape, lambda *_: (layer,) + (0,) * len(shape),
                        pipeline_mode=pl.Buffered(1))


def _ffn_kernel(x_ref, xnext_ref, ln_ref, wgu_ref, wd_ref, o_ref, act_ref, h_ref):
    tm = FFN_TM // 2

    @pl.when(pl.program_id(0) == 0)
    def _():
        h_ref[0] = _rms_rows(x_ref[0:tm, :], ln_ref[...]).astype(BF16)

    for half in range(2):
        rows = slice(half * tm, (half + 1) * tm)
        for c in range(D_FF // FFN_FC):
            cols = slice(c * FFN_FC, (c + 1) * FFN_FC)
            g = _dot(h_ref[half], wgu_ref[:, cols])
            u = _dot(h_ref[half], wgu_ref[:, D_FF + c * FFN_FC:D_FF + (c + 1) * FFN_FC])
            act_ref[:, cols] = (g * _sigmoid(g) * u).astype(BF16)
        ahead = x_ref[tm:2 * tm, :] if half == 0 else xnext_ref[0:tm, :]
        h_ref[1 - half] = _rms_rows(ahead, ln_ref[...]).astype(BF16)
        o_ref[rows, :] = x_ref[rows, :] + 0.5 * _dot(act_ref[...], wd_ref[...])


def _ffn(x2d, layer, ln, wgu, wd):
    n_tok = x2d.shape[0]
    return pl.pallas_call(
        _ffn_kernel,
        out_shape=jax.ShapeDtypeStruct(x2d.shape, F32),
        grid=(n_tok // FFN_TM,),
        in_specs=[
            pl.BlockSpec((FFN_TM, D_MODEL), lambda i: (i, 0)),
            pl.BlockSpec((FFN_TM, D_MODEL), lambda i: (jnp.minimum(i + 1, n_tok // FFN_TM - 1), 0)),
            _layer(layer, (1, D_MODEL)),
            _layer(layer, (D_MODEL, 2 * D_FF)),
            _layer(layer, (D_FF, D_MODEL)),
        ],
        out_specs=pl.BlockSpec((FFN_TM, D_MODEL), lambda i: (i, 0)),
        scratch_shapes=[pltpu.VMEM((FFN_TM // 2, D_FF), BF16),
                        pltpu.VMEM((2, FFN_TM // 2, D_MODEL), BF16)],
        compiler_params=pltpu.CompilerParams(
            dimension_semantics=("arbitrary",), vmem_limit_bytes=VMEM_LIMIT_BYTES),
        name="swiglu_halfstep",
    )(x2d, x2d, ln, wgu, wd)


def _head_rms(xq, gain, mean_blockdiag):
    x2 = xq * xq
    hi = x2.astype(BF16)
    lo = (x2 - hi.astype(F32)).astype(BF16)
    ms = _dot(jnp.concatenate([hi, lo], axis=1), mean_blockdiag)
    return xq * lax.rsqrt(ms + EPS) * gain


def _mix_phase(layer, t, at_seq_start, x_rows, rope, o_ref, rows_out, z_ref, h_ref, hnext_ref,
               znext_ref, normalise_ahead, sinks_ref, win_ref, poolw_ref, pscale_ref, wpb_ref,
               qg_ref, kg_ref, mean_ref, wab_ref, wout_ref, uext_ref, kd_ref, vd_ref, qs_ref, ao_ref):
    tm = MIX_TM
    n_blk = tm // BLOCK
    c_q, c_kv, c_gate = IN_SPLITS
    chunks = iter(range(c_gate // IN_CHUNK))
    gate_chunks = iter(range(c_gate // IN_CHUNK, IN_DIM // IN_CHUNK))

    def project_next(n):
        for _ in range(n):
            c = next(chunks)
            cols = slice(c * IN_CHUNK, (c + 1) * IN_CHUNK)
            znext_ref[:, cols] = _dot(hnext_ref[...], win_ref[:, cols])

    def gate_chunk():
        c = next(gate_chunks)
        return _sigmoid(_dot(h_ref[...], win_ref[:, c * IN_CHUNK:(c + 1) * IN_CHUNK]))

    rope_c = rope[:, 0:LANES]
    rope_s_hi = rope[:, LANES:2 * LANES]
    rope_s_lo = rope[:, 2 * LANES:3 * LANES]

    def apply_rope(xc):
        return (xc * rope_c + pltpu.roll(xc, LANES - ROT_DIM // 2, 1) * rope_s_hi
                + pltpu.roll(xc, ROT_DIM // 2, 1) * rope_s_lo)

    lane = lax.broadcasted_iota(jnp.int32, (1, LANES), 1)
    low_half = lane < HEAD_DIM

    gate_pool, gate_attn = [], []
    u = z_ref[:, 0:c_q]
    uext_ref[POOL_WMAX:POOL_WMAX + tm, :] = u
    tok = t * tm + lax.broadcasted_iota(jnp.int32, (tm, 1), 0)
    inv_tok = 1.0 / (tok + 1).astype(F32)
    d_groups = []
    for g, w in enumerate(POOL_WINDOWS):
        gate_pool.append(gate_chunk())
        lanes = slice(g * LANES, (g + 1) * LANES)
        win = uext_ref[:, lanes]
        span = 1
        while span < w:
            win = win + pltpu.roll(win, span, 0)
            span *= 2
        inv_count = jnp.where(tok + 1 < w, inv_tok, 1.0 / w)
        d_groups.append(win[POOL_WMAX:] * inv_count - u[:, lanes])
    uext_ref[0:POOL_WMAX, :] = uext_ref[tm:tm + POOL_WMAX, :]
    y_pairs = []
    for pair in range(2):
        d_pair = jnp.concatenate(d_groups[2 * pair:2 * pair + 2], axis=1).astype(BF16)
        y_pairs.append(_dot(d_pair, poolw_ref[pair]))
    y = jnp.concatenate(y_pairs, axis=1) * pscale_ref[...]
    gated_pool = jnp.concatenate(gate_pool, axis=1) * _dot(y.astype(BF16), wpb_ref[...])

    k = z_ref[:, c_kv:c_kv + KV_DIM]
    v = z_ref[:, c_kv + KV_DIM:c_gate]
    mean_k = mean_ref[0:LANES, 0:LANES]
    kn = apply_rope(_head_rms(k, kg_ref[...], jnp.concatenate([mean_k, mean_k], axis=0)))
    kn_sw = pltpu.roll(kn, HEAD_DIM, 1)
    v_sw = pltpu.roll(v, HEAD_DIM, 1)
    rows_new = slice(BLOCK, BLOCK + tm)
    kd_ref[0, rows_new, :] = jnp.where(low_half, kn, 0.0).astype(BF16)
    kd_ref[1, rows_new, :] = jnp.where(low_half, 0.0, kn_sw).astype(BF16)
    kd_ref[2, rows_new, :] = jnp.where(low_half, kn_sw, 0.0).astype(BF16)
    kd_ref[3, rows_new, :] = jnp.where(low_half, 0.0, kn).astype(BF16)
    vd_ref[0, rows_new, :] = jnp.where(low_half, v, v_sw).astype(BF16)
    vd_ref[1, rows_new, :] = jnp.where(low_half, v_sw, v).astype(BF16)

    for half in range(2):
        qn = _head_rms(z_ref[:, c_q + half * MXU_DIM:c_q + (half + 1) * MXU_DIM],
                       qg_ref[...], mean_ref[...])
        for sub in range(2):
            qc = apply_rope(qn[:, sub * LANES:(sub + 1) * LANES]).astype(BF16)
            for b in range(n_blk):
                qs_ref[b, half, sub * BLOCK:(sub + 1) * BLOCK, :] = qc[b * BLOCK:(b + 1) * BLOCK]
        project_next(1)

    g_rows = GQA_GROUP * BLOCK
    qi = lax.broadcasted_iota(jnp.int32, (g_rows, 2 * BLOCK), 0) & (BLOCK - 1)
    ki = lax.broadcasted_iota(jnp.int32, (g_rows, 2 * BLOCK), 1)
    band = (ki > qi) & (ki <= qi + BLOCK)
    head_of_row = lax.broadcasted_iota(jnp.int32, (g_rows, 1), 0) // BLOCK
    fillers = iter(("next", "gate", "next", "gate", "next", "gate", "gate"))
    for b in range(n_blk):
        keys = slice(b * BLOCK, (b + 2) * BLOCK)
        if b == 0 and at_seq_start:
            mask = band & (ki >= jnp.where(t > 0, 0, BLOCK))
        else:
            mask = band
        for hk in range(N_KV_HEADS):
            sink = jnp.zeros((g_rows, 1), F32)
            for j, head in enumerate((0, 2, 1, 3)):
                sink = jnp.where(head_of_row == j, sinks_ref[layer, hk * GQA_GROUP + head], sink)
            s = jnp.concatenate(
                [lax.dot_general(qs_ref[b, hk], kd_ref[2 * hk + hi, keys, :],
                                 (((1,), (1,)), ((), ())), preferred_element_type=F32)
                 for hi in range(2)], axis=0)
            s = jnp.where(mask, s, -jnp.inf)
            m = jnp.maximum(jnp.max(s, axis=-1, keepdims=True), sink)
            p = jnp.exp2(s - m)
            denom = jnp.sum(p, axis=-1, keepdims=True) + jnp.exp2(sink - m)
            o = _dot(p.astype(BF16), vd_ref[hk, keys, :]) / denom
            rows = slice(b * BLOCK, (b + 1) * BLOCK)
            for sub in range(2):
                pair = jnp.where(low_half, o[sub * BLOCK:(sub + 1) * BLOCK],
                                 o[(2 + sub) * BLOCK:(3 + sub) * BLOCK])
                col = (2 * hk + sub) * LANES
                ao_ref[rows, col:col + LANES] = pair.astype(BF16)
            filler = next(fillers, None)
            if filler == "gate":
                gate_attn.append(gate_chunk())
            elif filler == "next":
                project_next(1)
    kd_ref[:, 0:BLOCK, :] = kd_ref[:, tm:tm + BLOCK, :]
    vd_ref[:, 0:BLOCK, :] = vd_ref[:, tm:tm + BLOCK, :]
    b_attn = _dot(ao_ref[...], wab_ref[...])

    merged = gated_pool + jnp.concatenate(gate_attn, axis=1) * b_attn
    assert next(chunks, None) is None and next(gate_chunks, None) is None
    normalise_ahead()
    o_ref[rows_out, :] = x_rows[...] + _dot(merged.astype(BF16), wout_ref[...])


def _mixer_kernel(layer, tiles_per_seq, sinks_ref, x_ref, xnext_ref, ln_ref, win_ref, poolw_ref,
                  pscale_ref, wpb_ref, qg_ref, kg_ref, rope_ref, mean_ref, wab_ref, wout_ref,
                  o_ref, z_ref, h_ref, uext_ref, kd_ref, vd_ref, qs_ref, ao_ref):
    tm = MIX_TM
    i = pl.program_id(0)
    t0 = lax.rem(i * MIX_PHASES, tiles_per_seq)

    def norm_rows(src_ref, ph):
        rows = slice(ph * tm, (ph + 1) * tm)
        h_ref[ph] = _rms_rows(src_ref[rows, :], ln_ref[...]).astype(BF16)

    @pl.when(t0 == 0)
    def _():
        uext_ref[0:POOL_WMAX, :] = jnp.zeros((POOL_WMAX, POOL_DIM), F32)
        kd_ref[:, 0:BLOCK, :] = jnp.zeros((2 * N_KV_HEADS, BLOCK, LANES), BF16)
        vd_ref[:, 0:BLOCK, :] = jnp.zeros((N_KV_HEADS, BLOCK, LANES), BF16)

    @pl.when(i == 0)
    def _():
        norm_rows(x_ref, 0)
        for c in range(IN_SPLITS[-1] // IN_CHUNK):
            cols = slice(c * IN_CHUNK, (c + 1) * IN_CHUNK)
            z_ref[0, :, cols] = _dot(h_ref[0], win_ref[:, cols])
        norm_rows(x_ref, 1)

    for ph in range(MIX_PHASES):
        nxt = (ph + 1) % MIX_PHASES
        rows = slice(ph * tm, (ph + 1) * tm)
        _mix_phase(layer, t0 + ph, ph == 0, x_ref.at[rows], rope_ref.at[rows], o_ref, rows,
                   z_ref.at[ph], h_ref.at[ph], h_ref.at[nxt], z_ref.at[nxt],
                   lambda ph=ph: norm_rows(xnext_ref, ph),
                   sinks_ref, win_ref, poolw_ref, pscale_ref, wpb_ref, qg_ref, kg_ref, mean_ref,
                   wab_ref, wout_ref, uext_ref, kd_ref, vd_ref, qs_ref, ao_ref)


def _mixer(x2d, seq, layer, sinks, ln, win, poolw, pscale, wpb, qg, kg, rope_tab, mean_bd, wab, wout):
    tm = MIX_TM
    rows = MIX_PHASES * tm
    n_blk = tm // BLOCK
    n_steps = x2d.shape[0] // rows
    tiles_per_seq = seq // tm
    steps_per_seq = seq // rows
    block = pl.BlockSpec((rows, D_MODEL), lambda i: (i, 0))
    return pl.pallas_call(
        lambda *refs: _mixer_kernel(layer, tiles_per_seq, *refs),
        out_shape=jax.ShapeDtypeStruct(x2d.shape, F32),
        grid=(n_steps,),
        in_specs=[
            pl.BlockSpec(memory_space=pltpu.SMEM),
            block,
            pl.BlockSpec((rows, D_MODEL), lambda i: (jnp.minimum(i + 1, n_steps - 1), 0)),
            _layer(layer, (1, D_MODEL)),
            _layer(layer, (D_MODEL, IN_DIM)),
            _layer(layer, (2, MXU_DIM, MXU_DIM)),
            _layer(layer, (1, POOL_DIM)),
            _layer(layer, (POOL_DIM, D_MODEL)),
            _layer(layer, (1, MXU_DIM)),
            _layer(layer, (1, LANES)),
            pl.BlockSpec((rows, 3 * LANES), lambda i: (lax.rem(i, steps_per_seq), 0)),
            _resident((2 * MXU_DIM, MXU_DIM)),
            _layer(layer, (ATTN_DIM, D_MODEL)),
            _layer(layer, (D_MODEL, D_MODEL)),
        ],
        out_specs=block,
        scratch_shapes=[
            pltpu.VMEM((MIX_PHASES, tm, IN_SPLITS[-1]), F32),
            pltpu.VMEM((MIX_PHASES, tm, D_MODEL), BF16),
            pltpu.VMEM((POOL_WMAX + tm, POOL_DIM), F32),
            pltpu.VMEM((2 * N_KV_HEADS, BLOCK + tm, LANES), BF16),
            pltpu.VMEM((N_KV_HEADS, BLOCK + tm, LANES), BF16),
            pltpu.VMEM((n_blk, N_KV_HEADS, 2 * BLOCK, LANES), BF16),
            pltpu.VMEM((tm, ATTN_DIM), BF16),
        ],
        compiler_params=pltpu.CompilerParams(
            dimension_semantics=("arbitrary",), vmem_limit_bytes=VMEM_LIMIT_BYTES),
        name="gated_token_mixers",
    )(sinks, x2d, x2d, ln, win, poolw, pscale, wpb, qg, kg, rope_tab, mean_bd, wab, wout)


def _rope_table(seq):
    half = ROT_DIM // 2
    pos = jnp.arange(seq, dtype=F32)
    inv_freq = ROPE_THETA ** (-jnp.arange(0, ROT_DIM, 2, dtype=F32) / ROT_DIM)
    dim = jnp.arange(LANES) % HEAD_DIM
    ang = pos[:, None] * inv_freq[dim % half][None, :]
    cos, sin = jnp.cos(ang), jnp.sin(ang)
    c = jnp.where(dim < ROT_DIM, cos, 1.0)
    s_hi = jnp.where(dim < half, -sin, 0.0)
    s_lo = jnp.where((dim >= half) & (dim < ROT_DIM), sin, 0.0)
    return jnp.concatenate([c, s_hi, s_lo], axis=1)


def _block_diag_pairs(pool_w):
    depth = pool_w.shape[0]
    out = jnp.zeros((depth, 2, MXU_DIM, MXU_DIM), pool_w.dtype)
    out = out.at[:, :, :LANES, :LANES].set(pool_w[:, 0::2])
    return out.at[:, :, LANES:, LANES:].set(pool_w[:, 1::2])


def kernel(x, ln_ffn1, w_ffn1_gu, w_ffn1_down, ln_mix, w_in, pool_w, pool_scale, w_pool_branch,
           q_norm, k_norm, sinks, w_attn_branch, w_out, ln_ffn2, w_ffn2_gu, w_ffn2_down):
    bsz, seq, _ = x.shape
    depth = ln_ffn1.shape[0]
    assert seq % (MIX_PHASES * MIX_TM) == 0 and (bsz * seq) % FFN_TM == 0

    rope_tab = _rope_table(seq)
    head_id = (jnp.arange(2 * MXU_DIM) % MXU_DIM) // HEAD_DIM
    mean_bd = ((head_id[:, None] == head_id[None, :MXU_DIM]) * (1.0 / HEAD_DIM)).astype(BF16)

    wgu1, wd1 = w_ffn1_gu.astype(BF16), w_ffn1_down.astype(BF16)
    wgu2, wd2 = w_ffn2_gu.astype(BF16), w_ffn2_down.astype(BF16)
    win, wpb = w_in.astype(BF16), w_pool_branch.astype(BF16)
    wab, wout = w_attn_branch.astype(BF16), w_out.astype(BF16)
    poolw = _block_diag_pairs(pool_w).astype(BF16)
    ln1, lnm, ln2 = ln_ffn1[:, None, :], ln_mix[:, None, :], ln_ffn2[:, None, :]
    pscale = pool_scale[:, None, :]
    qg = jnp.tile(q_norm * (HEAD_DIM ** -0.5 * LOG2_E), (1, MXU_DIM // HEAD_DIM))[:, None, :]
    kg = jnp.tile(k_norm, (1, LANES // HEAD_DIM))[:, None, :]
    sinks = sinks * LOG2_E

    x2d = x.reshape(bsz * seq, D_MODEL)
    for l in range(depth):
        x2d = _ffn(x2d, l, ln1, wgu1, wd1)
        x2d = _mixer(x2d, seq, l, sinks, lnm, win, poolw, pscale, wpb,
                     qg, kg, rope_tab, mean_bd, wab, wout)
        x2d = _ffn(x2d, l, ln2, wgu2, wd2)
    return x2d.reshape(bsz, seq, D_MODEL)
```
